```python
import jax, jax.numpy as jnp
from jax import lax
import numpy as np

D_MODEL = 2048
BATCH = 8
SEQ = 4096
DEPTH = 4

HEAD_DIM = 64
ATTN_WIDTH = D_MODEL // 2
N_Q_HEADS = ATTN_WIDTH // HEAD_DIM
N_KV_HEADS = N_Q_HEADS // 4
Q_PER_KV = N_Q_HEADS // N_KV_HEADS
KV_WIDTH = N_KV_HEADS * HEAD_DIM
WINDOW = 128
ATTN_BLOCK = 128

SSM_WIDTH = D_MODEL - ATTN_WIDTH
SSM_HEAD_DIM = 64
SSM_HEADS = SSM_WIDTH // SSM_HEAD_DIM
SSM_GROUPS = 4
HEADS_PER_GROUP = SSM_HEADS // SSM_GROUPS
D_STATE = 128
CONV_WIDTH = 4
CHUNK = 128
CONV_CH = SSM_WIDTH + 2 * SSM_GROUPS * D_STATE

MIX_WIDTH = ATTN_WIDTH + SSM_WIDTH
SPLIT_SIZES = (ATTN_WIDTH, KV_WIDTH, KV_WIDTH, SSM_WIDTH, CONV_CH, SSM_HEADS)
IN_PROJ_WIDTH = sum(SPLIT_SIZES)
SPLIT_POINTS = tuple(int(s) for s in np.cumsum(SPLIT_SIZES)[:-1])

D_FF = ((8 * D_MODEL // 3 + 255) // 256) * 256
EPS = 1e-6
DT_MIN = 0.001
DT_MAX = 0.1

kernel_name = 'hybrid_swa_sink_mamba2_parallel_heads'


def rms_norm(x, w):
    xf = x.astype(jnp.float32)
    y = xf * lax.rsqrt(jnp.mean(xf * xf, axis=-1, keepdims=True) + EPS)
    return (y * w.astype(jnp.float32)).astype(x.dtype)


def alibi_slopes():
    h = np.arange(1, N_Q_HEADS + 1, dtype=np.float32)
    return jnp.asarray(np.power(np.float32(2.0), -8.0 * h / N_Q_HEADS), jnp.float32)


def sliding_window_attention(q, k, v, q_norm_w, k_norm_w, sinks):
    b, L = q.shape[0], q.shape[1]
    nb = L // ATTN_BLOCK
    q = rms_norm(q, q_norm_w)
    k = rms_norm(k, k_norm_w)
    qb = q.reshape(b, nb, ATTN_BLOCK, N_KV_HEADS, Q_PER_KV, HEAD_DIM)

    def band(t):
        tb = t.reshape(b, nb, ATTN_BLOCK, N_KV_HEADS, HEAD_DIM)
        prev = jnp.pad(tb, ((0, 0), (1, 0), (0, 0), (0, 0), (0, 0)))[:, :-1]
        return jnp.concatenate([prev, tb], axis=2)

    kb, vb = band(k), band(v)
    scores = jnp.einsum('bnqhgd,bnkhd->bnhgqk', qb, kb,
                        preferred_element_type=jnp.float32) * (HEAD_DIM ** -0.5)

    r = jnp.arange(ATTN_BLOCK)[:, None]
    j = jnp.arange(2 * ATTN_BLOCK)[None, :]
    dist = ATTN_BLOCK + r - j
    key_pos = (jnp.arange(nb)[:, None, None] - 1) * ATTN_BLOCK + j[None]
    valid = (dist >= 0)[None] & (dist < WINDOW)[None] & (key_pos >= 0)
    slopes = alibi_slopes().reshape(N_KV_HEADS, Q_PER_KV)
    bias = -slopes[:, :, None, None] * dist.astype(jnp.float32)
    scores = jnp.where(valid[None, :, None, None], scores + bias, -jnp.inf)

    sink = sinks.astype(jnp.float32).reshape(N_KV_HEADS, Q_PER_KV)[None, None, :, :, None, None]
    m = jnp.maximum(jnp.max(scores, axis=-1, keepdims=True), sink)
    p = jnp.exp(scores - m)
    probs = p / (jnp.sum(p, axis=-1, keepdims=True) + jnp.exp(sink - m))
    out = jnp.einsum('bnhgqk,bnkhd->bnqhgd', probs.astype(vb.dtype), vb)
    return out.reshape(b, L, ATTN_WIDTH)


def causal_depthwise_conv(u, w, bias):
    out = lax.conv_general_dilated(u, w[:, None, :], window_strides=(1,),
                                   padding=[(CONV_WIDTH - 1, 0)],
                                   dimension_numbers=('NWC', 'WIO', 'NWC'),
                                   feature_group_count=u.shape[-1])
    return out + bias


def ssd_chunked(xs, dt, a, bm, cm):
    b, L = xs.shape[0], xs.shape[1]
    nc = L // CHUNK
    x = xs.reshape(b, nc, CHUNK, SSM_GROUPS, HEADS_PER_GROUP, SSM_HEAD_DIM)
    dtc = dt.reshape(b, nc, CHUNK, SSM_GROUPS, HEADS_PER_GROUP)
    bc = bm.reshape(b, nc, CHUNK, SSM_GROUPS, D_STATE)
    cc = cm.reshape(b, nc, CHUNK, SSM_GROUPS, D_STATE)
    cum = jnp.cumsum(dtc * a.reshape(SSM_GROUPS, HEADS_PER_GROUP), axis=2)
    xdt = x * dtc[..., None]

    cum_t = jnp.moveaxis(cum, 2, -1)
    seg = cum_t[..., :, None] - cum_t[..., None, :]
    causal = jnp.tril(jnp.ones((CHUNK, CHUNK), dtype=bool))
    decay = jnp.exp(jnp.where(causal, seg, -jnp.inf))
    cb = jnp.einsum('bclgn,bcsgn->bcgls', cc, bc)
    y_diag = jnp.einsum('bcgkls,bcsgkp->bclgkp', cb[:, :, :, None] * decay, xdt)

    decay_to_end = jnp.exp(cum[:, :, -1:] - cum)
    states = jnp.einsum('bclgn,bclgk,bclgkp->bcgkpn', bc, decay_to_end, xdt)
    chunk_decay = jnp.exp(cum[:, :, -1])

    def step(h, inp):
        s_c, d_c = inp
        return h * d_c[..., None, None] + s_c, h

    init = jnp.zeros((b, SSM_GROUPS, HEADS_PER_GROUP, SSM_HEAD_DIM, D_STATE), states.dtype)
    _, prev = lax.scan(step, init, (jnp.moveaxis(states, 1, 0), jnp.moveaxis(chunk_decay, 1, 0)))
    prev = jnp.moveaxis(prev, 0, 1)

    y_off = jnp.einsum('bclgn,bcgkpn,bclgk->bclgkp', cc, prev, jnp.exp(cum))
    return (y_diag + y_off).reshape(b, L, SSM_HEADS, SSM_HEAD_DIM)


def mamba2_mixer(z, xbc, dt_raw, conv_w, conv_b, dt_bias, a_log, d_skip, norm_w):
    b, L = z.shape[0], z.shape[1]
    xbc = jax.nn.silu(causal_depthwise_conv(xbc, conv_w, conv_b)).astype(jnp.float32)
    xs = xbc[..., :SSM_WIDTH].reshape(b, L, SSM_HEADS, SSM_HEAD_DIM)
    bm = xbc[..., SSM_WIDTH:SSM_WIDTH + SSM_GROUPS * D_STATE].reshape(b, L, SSM_GROUPS, D_STATE)
    cm = xbc[..., SSM_WIDTH + SSM_GROUPS * D_STATE:].reshape(b, L, SSM_GROUPS, D_STATE)
    dt = jax.nn.softplus(dt_raw.astype(jnp.float32) + dt_bias.astype(jnp.float32))
    a = -jnp.exp(a_log.astype(jnp.float32))
    y = ssd_chunked(xs, dt, a, bm, cm) + d_skip.astype(jnp.float32)[:, None] * xs
    g = (y.reshape(b, L, SSM_WIDTH) * jax.nn.silu(z.astype(jnp.float32)))
    g = g.reshape(b, L, SSM_GROUPS, SSM_WIDTH // SSM_GROUPS)
    g = g * lax.rsqrt(jnp.mean(g * g, axis=-1, keepdims=True) + EPS)
    g = g.reshape(b, L, SSM_WIDTH) * norm_w.astype(jnp.float32)
    return g.astype(z.dtype)


def setup_inputs(seed: int = 0) -> dict:
    key = jax.random.key(seed)
    ks = jax.random.split(key, 20)
    f32 = jnp.float32

    def nrm(k, shape, scale):
        return jax.random.normal(k, shape, f32) * scale

    log_dt = jax.random.uniform(ks[9], (DEPTH, SSM_HEADS), f32,
                                float(np.log(DT_MIN)), float(np.log(DT_MAX)))
    dt0 = jnp.exp(log_dt)
    dt_bias = dt0 + jnp.log(-jnp.expm1(-dt0))
    return {
        'x': nrm(ks[0], (BATCH, SEQ, D_MODEL), 1.0),
        'attn_norm_w': 1.0 + nrm(ks[1], (DEPTH, D_MODEL), 0.02),
        'w_in': nrm(ks[2], (DEPTH, D_MODEL, IN_PROJ_WIDTH), D_MODEL ** -0.5),
        'q_norm_w': 1.0 + nrm(ks[3], (DEPTH, HEAD_DIM), 0.02),
        'k_norm_w': 1.0 + nrm(ks[4], (DEPTH, HEAD_DIM), 0.02),
        'sinks': nrm(ks[5], (DEPTH, N_Q_HEADS), 0.5),
        'conv_w': nrm(ks[6], (DEPTH, CONV_WIDTH, CONV_CH), CONV_WIDTH ** -0.5),
        'conv_b': nrm(ks[7], (DEPTH, CONV_CH), 0.02),
        'dt_bias': dt_bias,
        'a_log': jnp.log(jax.random.uniform(ks[8], (DEPTH, SSM_HEADS), f32, 1.0, 16.0)),
        'd_skip': 1.0 + nrm(ks[10], (DEPTH, SSM_HEADS), 0.1),
        'ssm_norm_w': 1.0 + nrm(ks[11], (DEPTH, SSM_WIDTH), 0.02),
        'w_out': nrm(ks[12], (DEPTH, MIX_WIDTH, D_MODEL), MIX_WIDTH ** -0.5),
        'ffn_norm_w': 1.0 + nrm(ks[13], (DEPTH, D_MODEL), 0.02),
        'w_gate': nrm(ks[14], (DEPTH, D_MODEL, D_FF), D_MODEL ** -0.5),
        'w_up': nrm(ks[15], (DEPTH, D_MODEL, D_FF), D_MODEL ** -0.5),
        'w_down': nrm(ks[16], (DEPTH, D_FF, D_MODEL), D_FF ** -0.5),
    }


def reference(x, attn_norm_w, w_in, q_norm_w, k_norm_w, sinks, conv_w, conv_b, dt_bias,
              a_log, d_skip, ssm_norm_w, w_out, ffn_norm_w, w_gate, w_up, w_down):
    b, L = x.shape[0], x.shape[1]
    for l in range(DEPTH):
        h = rms_norm(x, attn_norm_w[l])
        proj = h @ w_in[l]
        q, k, v, z, xbc, dt_raw = jnp.split(proj, SPLIT_POINTS, axis=-1)
        attn = sliding_window_attention(
            q.reshape(b, L, N_Q_HEADS, HEAD_DIM),
            k.reshape(b, L, N_KV_HEADS, HEAD_DIM),
            v.reshape(b, L, N_KV_HEADS, HEAD_DIM),
            q_norm_w[l], k_norm_w[l], sinks[l])
        ssm = mamba2_mixer(z, xbc, dt_raw, conv_w[l], conv_b[l], dt_bias[l], a_log[l],
                           d_skip[l], ssm_norm_w[l])
        mixed = jnp.concatenate([attn.astype(x.dtype), ssm.astype(x.dtype)], axis=-1) @ w_out[l]
        x = (x + mixed).astype(x.dtype)
        h = rms_norm(x, ffn_norm_w[l])
        ff = (jax.nn.silu(h @ w_gate[l]) * (h @ w_up[l])) @ w_down[l]
        x = (x + ff).astype(x.dtype)
    return x
```

```python
import functools

import numpy as np
import jax
import jax.numpy as jnp
from jax import lax
from jax.experimental import pallas as pl
from jax.experimental.pallas import tpu as pltpu

F32 = jnp.float32
BF16 = jnp.bfloat16

D_MODEL = 2048
HEAD_DIM = 64
ATTN_WIDTH = 1024
N_Q_HEADS = ATTN_WIDTH // HEAD_DIM
N_KV_HEADS = 4
Q_PER_KV = N_Q_HEADS // N_KV_HEADS
KV_WIDTH = N_KV_HEADS * HEAD_DIM
BLK = 128
SSM_WIDTH = 1024
SSM_HEADS = SSM_WIDTH // HEAD_DIM
SSM_GROUPS = 4
D_STATE = 128
CONV_WIDTH = 4
CONV_CH = SSM_WIDTH + 2 * SSM_GROUPS * D_STATE
D_FF = 5632
EPS = 1e-6

LANES = 128
SUBLANES = 8
HALF = LANES // 2
MXU_DIM = 256
VMEM_LIMIT = 56 * 1024 * 1024

Z_OFF = ATTN_WIDTH
XBC_OFF = Z_OFF + SSM_WIDTH
K_OFF = XBC_OFF + CONV_CH
V_OFF = K_OFF + KV_WIDTH
PROJ_WIDTH = V_OFF + KV_WIDTH

ROW_TILE = 512
INPROJ_COL_TILE = 1152
OUTPROJ_COL_TILE = 1024
FF_TILE = 512
MIX_ROWS = 512

ALIBI_SLOPES = tuple(float(s) for s in np.power(
    np.float32(2.0), -8.0 * np.arange(1, N_Q_HEADS + 1, dtype=np.float32) / N_Q_HEADS))


def _params(n_axes):
    return pltpu.CompilerParams(dimension_semantics=("arbitrary",) * n_axes,
                                vmem_limit_bytes=VMEM_LIMIT)


def _split_bf16(x, parts):
    out = []
    for _ in range(parts - 1):
        hi = x.astype(BF16)
        out.append(hi)
        x = x - hi.astype(F32)
    out.append(x.astype(BF16))
    return out


def _dot(a, b):
    return jnp.dot(a, b, preferred_element_type=F32)


def _dot_nt(a, b):
    return lax.dot_general(a, b, (((1,), (1,)), ((), ())), preferred_element_type=F32)


def _norm_inproj_kernel(x_ref, nw_ref, w_ref, wdt_ref, o_ref, odt_ref, h_ref):
    @pl.when(pl.program_id(1) == 0)
    def _():
        x = x_ref[...]
        ms = jnp.mean(x * x, axis=-1, keepdims=True)
        h = (x * lax.rsqrt(ms + EPS) * nw_ref[...]).astype(BF16)
        h_ref[...] = h
        odt_ref[...] = _dot(h, wdt_ref[...])

    o_ref[...] = _dot(h_ref[...], w_ref[...]).astype(o_ref.dtype)


def _norm_inproj(x, norm_w, w_main, w_dt):
    m, d = x.shape
    n = w_main.shape[1]
    tm, tn = min(ROW_TILE, m), INPROJ_COL_TILE
    return pl.pallas_call(
        _norm_inproj_kernel,
        grid=(m // tm, n // tn),
        in_specs=[pl.BlockSpec((tm, d), lambda i, j: (i, 0)),
                  pl.BlockSpec((1, d), lambda i, j: (0, 0)),
                  pl.BlockSpec((d, tn), lambda i, j: (0, j)),
                  pl.BlockSpec((d, LANES), lambda i, j: (0, 0))],
        out_specs=[pl.BlockSpec((tm, tn), lambda i, j: (i, j)),
                   pl.BlockSpec((tm, LANES), lambda i, j: (i, 0))],
        out_shape=[jax.ShapeDtypeStruct((m, n), BF16),
                   jax.ShapeDtypeStruct((m, LANES), F32)],
        scratch_shapes=[pltpu.VMEM((tm, d), BF16)],
        compiler_params=_params(2),
        name="norm_inproj",
    )(x, norm_w, w_main, w_dt)


def _head_sumsq(x):
    r = lax.broadcasted_iota(jnp.int32, (MXU_DIM, MXU_DIM), 0) // HEAD_DIM
    c = lax.broadcasted_iota(jnp.int32, (MXU_DIM, MXU_DIM), 1) // HEAD_DIM
    seg = (r == c).astype(BF16)
    outs = []
    for g in range(x.shape[1] // MXU_DIM):
        xg = x[:, g * MXU_DIM:(g + 1) * MXU_DIM]
        hi, lo = _split_bf16(xg * xg, 2)
        outs.append(_dot(hi, seg) + _dot(lo, seg))
    return outs[0] if len(outs) == 1 else jnp.concatenate(outs, axis=1)


def _attn_kernel(sink_ref, q_ref, kc_ref, vc_ref, kp_ref, vp_ref, qw_ref, kw_ref, o_ref,
                 qs_ref, kd_ref, vd_ref, *, rows, blocks_per_seq):
    step = pl.program_id(0)
    nblk = rows // BLK
    lo_half = lax.broadcasted_iota(jnp.int32, (1, LANES), 1) < HALF

    k_all = jnp.concatenate([kp_ref[...], kc_ref[...]], axis=0).astype(F32)
    v_all = jnp.concatenate([vp_ref[...], vc_ref[...]], axis=0).astype(F32)
    k_all = k_all * lax.rsqrt(_head_sumsq(k_all) * (1.0 / HEAD_DIM) + EPS) * kw_ref[...]
    for src, dst in ((k_all, kd_ref), (v_all, vd_ref)):
        for p in range(KV_WIDTH // LANES):
            a = src[:, p * LANES:(p + 1) * LANES]
            b = pltpu.roll(a, HALF, 1)
            dst[2 * p] = jnp.where(lo_half, a, b).astype(BF16)
            dst[2 * p + 1] = jnp.where(lo_half, b, a).astype(BF16)

    q = q_ref[...].astype(F32)
    q = q * lax.rsqrt(_head_sumsq(q) * (1.0 / HEAD_DIM) + EPS) * qw_ref[...]
    lo_q = (lax.broadcasted_iota(jnp.int32, (1, ATTN_WIDTH), 1) & HALF) == 0
    qs_ref[0] = jnp.where(lo_q, q, 0.0).astype(BF16)
    qs_ref[1] = jnp.where(lo_q, 0.0, q).astype(BF16)

    r_idx = lax.broadcasted_iota(jnp.int32, (BLK, 2 * BLK), 0)
    c_idx = lax.broadcasted_iota(jnp.int32, (BLK, 2 * BLK), 1)
    band = (c_idx > r_idx) & (c_idx <= r_idx + BLK)
    row_f = lax.broadcasted_iota(jnp.int32, (BLK, 1), 0).astype(F32)
    col_f = lax.broadcasted_iota(jnp.int32, (1, 2 * BLK), 1).astype(F32)

    def block(t, carry):
        r0 = pl.multiple_of(t * BLK, BLK)
        first = ((step * nblk + t) % blocks_per_seq) == 0
        valid = band & (c_idx >= jnp.where(first, BLK, 0))
        for j in range(N_KV_HEADS):
            c0 = j * Q_PER_KV * HEAD_DIM
            lhs = jnp.concatenate(
                [qs_ref[half, pl.ds(r0, BLK), c0 + grp * LANES:c0 + (grp + 1) * LANES]
                 for grp in range(2) for half in range(2)], axis=0)
            kj = kd_ref[j, pl.ds(r0, 2 * BLK), :]
            vj = vd_ref[j, pl.ds(r0, 2 * BLK), :]
            s = _dot_nt(lhs, kj)
            probs, inv_den = [], []
            for hh in range(Q_PER_KV):
                h = j * Q_PER_KV + hh
                slope = ALIBI_SLOPES[h]
                sh = s[hh * BLK:(hh + 1) * BLK] + slope * col_f
                sh = jnp.where(valid, sh, -jnp.inf)
                sink = sink_ref[h] + slope * (row_f + float(BLK))
                mx = jnp.maximum(jnp.max(sh, axis=-1, keepdims=True), sink)
                p = jnp.exp(sh - mx)
                den = jnp.sum(p, axis=-1, keepdims=True) + jnp.exp(sink - mx)
                probs.append(p.astype(BF16))
                inv_den.append(1.0 / den)
            o = _dot(jnp.concatenate(probs, axis=0), vj)
            for grp in range(2):
                oa = o[(2 * grp) * BLK:(2 * grp + 1) * BLK] * inv_den[2 * grp]
                ob = o[(2 * grp + 1) * BLK:(2 * grp + 2) * BLK] * inv_den[2 * grp + 1]
                o_ref[pl.ds(r0, BLK), c0 + grp * LANES:c0 + (grp + 1) * LANES] = (
                    jnp.where(lo_half, oa, ob).astype(o_ref.dtype))
        return carry

    lax.fori_loop(0, nblk, block, 0)


def _attention(proj, q_norm_w, k_norm_w, sinks, *, seq):
    m = proj.shape[0]
    rows = min(MIX_ROWS, seq)
    nblk = rows // BLK
    qw = (jnp.tile(q_norm_w.astype(F32), N_Q_HEADS) * (HEAD_DIM ** -0.5))[None]
    kw = jnp.tile(k_norm_w.astype(F32), N_KV_HEADS)[None]
    kb, vb = K_OFF // KV_WIDTH, V_OFF // KV_WIDTH
    prev = lambda i: jnp.maximum(i * nblk - 1, 0)
    return pl.pallas_call(
        functools.partial(_attn_kernel, rows=rows, blocks_per_seq=seq // BLK),
        grid=(m // rows,),
        in_specs=[pl.BlockSpec(memory_space=pltpu.SMEM),
                  pl.BlockSpec((rows, ATTN_WIDTH), lambda i: (i, 0)),
                  pl.BlockSpec((rows, KV_WIDTH), lambda i: (i, kb)),
                  pl.BlockSpec((rows, KV_WIDTH), lambda i: (i, vb)),
                  pl.BlockSpec((BLK, KV_WIDTH), lambda i: (prev(i), kb)),
                  pl.BlockSpec((BLK, KV_WIDTH), lambda i: (prev(i), vb)),
                  pl.BlockSpec((1, ATTN_WIDTH), lambda i: (0, 0)),
                  pl.BlockSpec((1, KV_WIDTH), lambda i: (0, 0))],
        out_specs=pl.BlockSpec((rows, ATTN_WIDTH), lambda i: (i, 0)),
        out_shape=jax.ShapeDtypeStruct((m, ATTN_WIDTH), BF16),
        scratch_shapes=[pltpu.VMEM((2, rows, ATTN_WIDTH), BF16),
                        pltpu.VMEM((N_KV_HEADS, rows + BLK, LANES), BF16),
                        pltpu.VMEM((N_KV_HEADS, rows + BLK, LANES), BF16)],
        compiler_params=_params(1),
        name="swa_attention",
    )(sinks.astype(F32), proj, proj, proj, proj, proj, qw, kw)


def _sigmoid(x):
    return 1.0 / (1.0 + jnp.exp(-x))


def _ssd_kernel(z_ref, xbc_ref, dt_ref, cw_ref, cb_ref, dtb_ref, alog_ref, dskip_ref, nw_ref, o_ref,
                xbuf_ref, u_ref, h_ref, *, rows):
    nchunk = rows // BLK
    pad = SUBLANES
    first_step = pl.program_id(1) == 0

    @pl.when(first_step)
    def _():
        xbuf_ref[0:pad, :] = jnp.zeros((pad, CONV_CH), F32)
        h_ref[...] = jnp.zeros_like(h_ref)

    @pl.when(jnp.logical_not(first_step))
    def _():
        xbuf_ref[0:pad, :] = xbuf_ref[rows:rows + pad, :]

    xbuf_ref[pad:pad + rows, :] = xbc_ref[...].astype(F32)

    for c in range(nchunk):
        acc = cb_ref[...]
        for k in range(CONV_WIDTH):
            off = pad + c * BLK - (CONV_WIDTH - 1 - k)
            acc = acc + cw_ref[k:k + 1, :] * xbuf_ref[off:off + BLK, :]
        u_ref[c * BLK:(c + 1) * BLK, :] = acc * _sigmoid(acc)

    lane = lax.broadcasted_iota(jnp.int32, (1, LANES), 1)
    lo_half = lane < HALF
    r_idx = lax.broadcasted_iota(jnp.int32, (BLK, BLK), 0)
    c_idx = lax.broadcasted_iota(jnp.int32, (BLK, BLK), 1)
    causal = c_idx <= r_idx
    tri = causal.astype(BF16)
    expand = (lax.broadcasted_iota(jnp.int32, (LANES, SSM_WIDTH), 0)
              == lax.broadcasted_iota(jnp.int32, (LANES, SSM_WIDTH), 1) // HEAD_DIM).astype(BF16)
    a_neg = jnp.where(lane < SSM_HEADS, -jnp.exp(alog_ref[...]), 0.0)

    def chunk(t, carry):
        r0 = pl.multiple_of(t * BLK, BLK)
        rws = pl.ds(r0, BLK)
        v = dt_ref[rws, :] + dtb_ref[...]
        dt = jnp.maximum(v, 0.0) + jnp.log1p(jnp.exp(-jnp.abs(v)))
        da = dt * a_neg
        cum = sum(_dot(tri, part) for part in _split_bf16(da, 3))
        cum_t = cum.T[0:SSM_HEADS]
        dt_t = dt.T[0:SSM_HEADS]
        w_t = jnp.exp(cum_t[:, BLK - 1:BLK] - cum_t) * dt_t
        p_hi, p_lo = _split_bf16(jnp.exp(cum), 2)
        p_exp = _dot(p_hi, expand) + _dot(p_lo, expand)

        for g in range(SSM_GROUPS):
            b_g = u_ref[rws, SSM_WIDTH + g * D_STATE:SSM_WIDTH + (g + 1) * D_STATE]
            c_g = u_ref[rws, SSM_WIDTH + (SSM_GROUPS + g) * D_STATE:SSM_WIDTH + (SSM_GROUPS + g + 1) * D_STATE]
            c_bf = c_g.astype(BF16)
            cb = _dot_nt(c_bf, b_g.astype(BF16))
            b_t = b_g.T
            gated = []
            for pair in range(2 * g, 2 * g + 2):
                cols = slice(pair * LANES, (pair + 1) * LANES)
                x_f = u_ref[rws, cols]
                x_bf = x_f.astype(BF16)
                y_k, s_k = [], []
                for k in (2 * pair, 2 * pair + 1):
                    seg = cum[:, k:k + 1] - cum_t[k:k + 1, :]
                    decay = jnp.exp(jnp.where(causal, seg, -jnp.inf))
                    y_k.append(_dot((cb * decay * dt_t[k:k + 1, :]).astype(BF16), x_bf))
                    s_k.append(_dot((b_t * w_t[k:k + 1, :]).astype(BF16), x_bf))
                h_prev = h_ref[pair]
                y = (jnp.where(lo_half, y_k[0], y_k[1])
                     + _dot(c_bf, h_prev.astype(BF16)) * p_exp[:, cols]
                     + dskip_ref[:, cols] * x_f)
                h_ref[pair] = h_prev * p_exp[BLK - 1:BLK, cols] + jnp.where(lo_half, s_k[0], s_k[1])
                zf = z_ref[rws, cols].astype(F32)
                gated.append(y * (zf * _sigmoid(zf)))
            gg = jnp.concatenate(gated, axis=1)
            ms = jnp.mean(gg * gg, axis=-1, keepdims=True)
            gcols = slice(2 * g * LANES, (2 * g + 2) * LANES)
            o_ref[rws, gcols] = (gg * lax.rsqrt(ms + EPS) * nw_ref[:, gcols]).astype(o_ref.dtype)
        return carry

    lax.fori_loop(0, nchunk, chunk, 0)


def _ssd(proj, dt_raw, conv_w, conv_b, dt_bias, a_log, d_skip, norm_w, *, batch, seq):
    rows = min(MIX_ROWS, seq)
    steps = seq // rows
    pad_heads = lambda v: jnp.pad(v.astype(F32), (0, LANES - SSM_HEADS))[None]
    row = lambda b, c: b * steps + c
    return pl.pallas_call(
        functools.partial(_ssd_kernel, rows=rows),
        grid=(batch, steps),
        in_specs=[pl.BlockSpec((rows, SSM_WIDTH), lambda b, c: (row(b, c), Z_OFF // SSM_WIDTH)),
                  pl.BlockSpec((rows, CONV_CH), lambda b, c: (row(b, c), XBC_OFF // CONV_CH)),
                  pl.BlockSpec((rows, LANES), lambda b, c: (row(b, c), 0)),
                  pl.BlockSpec((CONV_WIDTH, CONV_CH), lambda b, c: (0, 0)),
                  pl.BlockSpec((1, CONV_CH), lambda b, c: (0, 0)),
                  pl.BlockSpec((1, LANES), lambda b, c: (0, 0)),
                  pl.BlockSpec((1, LANES), lambda b, c: (0, 0)),
                  pl.BlockSpec((1, SSM_WIDTH), lambda b, c: (0, 0)),
                  pl.BlockSpec((1, SSM_WIDTH), lambda b, c: (0, 0))],
        out_specs=pl.BlockSpec((rows, SSM_WIDTH), lambda b, c: (row(b, c), 0)),
        out_shape=jax.ShapeDtypeStruct((batch * seq, SSM_WIDTH), BF16),
        scratch_shapes=[pltpu.VMEM((rows + SUBLANES, CONV_CH), F32),
                        pltpu.VMEM((rows, CONV_CH), F32),
                        pltpu.VMEM((SSM_HEADS // 2, D_STATE, LANES), F32)],
        compiler_params=_params(2),
        name="ssd_mixer",
    )(proj, proj, dt_raw, conv_w.astype(F32), conv_b.astype(F32)[None], pad_heads(dt_bias),
      pad_heads(a_log), jnp.repeat(d_skip.astype(F32), HEAD_DIM)[None], norm_w.astype(F32)[None])


def _outproj_kernel(x_ref, a_ref, s_ref, wa_ref, ws_ref, o_ref):
    o_ref[...] = x_ref[...] + _dot(a_ref[...], wa_ref[...]) + _dot(s_ref[...], ws_ref[...])


def _out_proj(x, attn, ssm, w_attn, w_ssm):
    m, d = x.shape
    tm, tn = min(ROW_TILE, m), OUTPROJ_COL_TILE
    return pl.pallas_call(
        _outproj_kernel,
        grid=(m // tm, d // tn),
        in_specs=[pl.BlockSpec((tm, tn), lambda i, j: (i, j)),
                  pl.BlockSpec((tm, ATTN_WIDTH), lambda i, j: (i, 0)),
                  pl.BlockSpec((tm, SSM_WIDTH), lambda i, j: (i, 0)),
                  pl.BlockSpec((ATTN_WIDTH, tn), lambda i, j: (0, j)),
                  pl.BlockSpec((SSM_WIDTH, tn), lambda i, j: (0, j))],
        out_specs=pl.BlockSpec((tm, tn), lambda i, j: (i, j)),
        out_shape=jax.ShapeDtypeStruct((m, d), F32),
        compiler_params=_params(2),
        name="out_proj",
    )(x, attn, ssm, w_attn, w_ssm)


def _ffn_kernel(x_ref, nw_ref, wg_ref, wu_ref, wd_ref, o_ref, h_ref):
    @pl.when(pl.program_id(1) == 0)
    def _():
        x = x_ref[...]
        ms = jnp.mean(x * x, axis=-1, keepdims=True)
        h_ref[...] = (x * lax.rsqrt(ms + EPS) * nw_ref[...]).astype(BF16)
        o_ref[...] = x

    h = h_ref[...]
    gate = _dot(h, wg_ref[...])
    up = _dot(h, wu_ref[...])
    act = (gate * _sigmoid(gate) * up).astype(BF16)
    o_ref[...] += _dot(act, wd_ref[...])


def _ffn(x, norm_w, w_gate, w_up, w_down):
    m, d = x.shape
    f = w_gate.shape[1]
    tm, tf = min(ROW_TILE, m), FF_TILE
    return pl.pallas_call(
        _ffn_kernel,
        grid=(m // tm, f // tf),
        in_specs=[pl.BlockSpec((tm, d), lambda i, j: (i, 0)),
                  pl.BlockSpec((1, d), lambda i, j: (0, 0)),
                  pl.BlockSpec((d, tf), lambda i, j: (0, j)),
                  pl.BlockSpec((d, tf), lambda i, j: (0, j)),
                  pl.BlockSpec((tf, d), lambda i, j: (j, 0))],
        out_specs=pl.BlockSpec((tm, d), lambda i, j: (i, 0)),
        out_shape=jax.ShapeDtypeStruct((m, d), F32),
        scratch_shapes=[pltpu.VMEM((tm, d), BF16)],
        compiler_params=_params(2),
        name="swiglu_ffn",
    )(x, norm_w, w_gate, w_up, w_down)


def _layer(x, p, *, batch, seq):
    (attn_norm_w, w_in, q_norm_w, k_norm_w, sinks, conv_w, conv_b, dt_bias, a_log, d_skip,
     ssm_norm_w, w_out, ffn_norm_w, w_gate, w_up, w_down) = p
    q0, k0, v0, z0, x0, t0 = np.cumsum((0, ATTN_WIDTH, KV_WIDTH, KV_WIDTH, SSM_WIDTH, CONV_CH))
    w_main = jnp.concatenate([w_in[:, q0:k0], w_in[:, z0:x0], w_in[:, x0:t0], w_in[:, k0:v0], w_in[:, v0:z0]],
                             axis=1).astype(BF16)
    w_dt = jnp.pad(w_in[:, t0:], ((0, 0), (0, LANES - SSM_HEADS))).astype(BF16)

    proj, dt_raw = _norm_inproj(x, attn_norm_w.astype(F32)[None], w_main, w_dt)
    attn = _attention(proj, q_norm_w, k_norm_w, sinks, seq=seq)
    ssm = _ssd(proj, dt_raw, conv_w, conv_b, dt_bias, a_log, d_skip, ssm_norm_w, batch=batch, seq=seq)
    x = _out_proj(x, attn, ssm, w_out[:ATTN_WIDTH].astype(BF16), w_out[ATTN_WIDTH:].astype(BF16))
    return _ffn(x, ffn_norm_w.astype(F32)[None], w_gate.astype(BF16), w_up.astype(BF16), w_down.astype(BF16))


def kernel(x, attn_norm_w, w_in, q_norm_w, k_norm_w, sinks, conv_w, conv_b, dt_bias, a_log, d_skip,
           ssm_norm_w, w_out, ffn_norm_w, w_gate, w_up, w_down):
    batch, seq, d = x.shape
    stacked = (attn_norm_w, w_in, q_norm_w, k_norm_w, sinks, conv_w, conv_b, dt_bias, a_log, d_skip,
               ssm_norm_w, w_out, ffn_norm_w, w_gate, w_up, w_down)
    h = x.reshape(batch * seq, d)
    for layer in range(attn_norm_w.shape[0]):
        h = _layer(h, tuple(t[layer] for t in stacked), batch=batch, seq=seq)
    return h.reshape(batch, seq, d)
```

```python
import functools

import numpy as np
import jax
import jax.numpy as jnp
from jax import lax
from jax.experimental import pallas as pl
from jax.experimental.pallas import tpu as pltpu

F32 = jnp.float32
BF16 = jnp.bfloat16

D_MODEL = 2048
HEAD_DIM = 64
ATTN_WIDTH = 1024
N_Q_HEADS = ATTN_WIDTH // HEAD_DIM
N_KV_HEADS = 4
Q_PER_KV = N_Q_HEADS // N_KV_HEADS
KV_WIDTH = N_KV_HEADS * HEAD_DIM
BLK = 128
SSM_WIDTH = 1024
SSM_HEADS = SSM_WIDTH // HEAD_DIM
SSM_GROUPS = 4
D_STATE = 128
CONV_WIDTH = 4
CONV_CH = SSM_WIDTH + 2 * SSM_GROUPS * D_STATE
D_FF = 5632
EPS = 1e-6

LANES = 128
SUBLANES = 8
HALF = LANES // 2
MXU_DIM = 256
VMEM_LIMIT = 56 * 1024 * 1024

Z_OFF = ATTN_WIDTH
XBC_OFF = Z_OFF + SSM_WIDTH
K_OFF = XBC_OFF + CONV_CH
V_OFF = K_OFF + KV_WIDTH
PROJ_WIDTH = V_OFF + KV_WIDTH

ROW_TILE = 512
INPROJ_COL_TILE = 1536
FF_TILE = 512
MIX_ROWS = 512
CONV_SLAB = 512

ALIBI_SLOPES = tuple(float(s) for s in np.power(
    np.float32(2.0), -8.0 * np.arange(1, N_Q_HEADS + 1, dtype=np.float32) / N_Q_HEADS))


def _params(n_axes):
    return pltpu.CompilerParams(dimension_semantics=("arbitrary",) * n_axes,
                                vmem_limit_bytes=VMEM_LIMIT)


def _split_bf16(x, parts):
    out = []
    for _ in range(parts - 1):
        hi = x.astype(BF16)
        out.append(hi)
        x = x - hi.astype(F32)
    out.append(x.astype(BF16))
    return out


def _dot(a, b):
    return jnp.dot(a, b, preferred_element_type=F32)


def _dot_nt(a, b):
    return lax.dot_general(a, b, (((1,), (1,)), ((), ())), preferred_element_type=F32)


def _norm_inproj_kernel(x_ref, nw_ref, w_ref, wdt_ref, o_ref, odt_ref, h_ref):
    x = x_ref[...]
    ms = jnp.mean(x * x, axis=-1, keepdims=True)
    h_ref[...] = (x * lax.rsqrt(ms + EPS) * nw_ref[...]).astype(BF16)
    odt_ref[...] = _dot(h_ref[...], wdt_ref[...])
    for n in range(o_ref.shape[1] // INPROJ_COL_TILE):
        cols = slice(n * INPROJ_COL_TILE, (n + 1) * INPROJ_COL_TILE)
        o_ref[:, cols] = _dot(h_ref[...], w_ref[:, cols]).astype(o_ref.dtype)


def _resident(shape):
    return pl.BlockSpec(shape, lambda i: (0,) * len(shape), pipeline_mode=pl.Buffered(1))


def _norm_inproj(x, norm_w, w_main, w_dt):
    m, d = x.shape
    n = w_main.shape[1]
    tm = min(ROW_TILE, m)
    return pl.pallas_call(
        _norm_inproj_kernel,
        grid=(m // tm,),
        in_specs=[pl.BlockSpec((tm, d), lambda i: (i, 0)),
                  _resident((1, d)),
                  _resident((d, n)),
                  _resident((d, LANES))],
        out_specs=[pl.BlockSpec((tm, n), lambda i: (i, 0)),
                   pl.BlockSpec((tm, LANES), lambda i: (i, 0))],
        out_shape=[jax.ShapeDtypeStruct((m, n), BF16),
                   jax.ShapeDtypeStruct((m, LANES), F32)],
        scratch_shapes=[pltpu.VMEM((tm, d), BF16)],
        compiler_params=_params(1),
        name="norm_inproj",
    )(x, norm_w, w_main, w_dt)


def _head_sumsq(x):
    r = lax.broadcasted_iota(jnp.int32, (MXU_DIM, MXU_DIM), 0) // HEAD_DIM
    c = lax.broadcasted_iota(jnp.int32, (MXU_DIM, MXU_DIM), 1) // HEAD_DIM
    seg = (r == c).astype(BF16)
    outs = []
    for g in range(x.shape[1] // MXU_DIM):
        xg = x[:, g * MXU_DIM:(g + 1) * MXU_DIM]
        hi, lo = _split_bf16(xg * xg, 2)
        outs.append(_dot(hi, seg) + _dot(lo, seg))
    return outs[0] if len(outs) == 1 else jnp.concatenate(outs, axis=1)


def _attn_kernel(sink_ref, q_ref, kc_ref, vc_ref, kp_ref, vp_ref, qw_ref, kw_ref, o_ref,
                 qs_ref, kd_ref, vd_ref, *, rows, blocks_per_seq):
    step = pl.program_id(0)
    nblk = rows // BLK
    lo_half = lax.broadcasted_iota(jnp.int32, (1, LANES), 1) < HALF

    k_all = jnp.concatenate([kp_ref[...], kc_ref[...]], axis=0).astype(F32)
    v_all = jnp.concatenate([vp_ref[...], vc_ref[...]], axis=0).astype(F32)
    k_all = k_all * lax.rsqrt(_head_sumsq(k_all) * (1.0 / HEAD_DIM) + EPS) * kw_ref[...]
    for src, dst in ((k_all, kd_ref), (v_all, vd_ref)):
        for p in range(KV_WIDTH // LANES):
            a = src[:, p * LANES:(p + 1) * LANES]
            b = pltpu.roll(a, HALF, 1)
            dst[2 * p] = jnp.where(lo_half, a, b).astype(BF16)
            dst[2 * p + 1] = jnp.where(lo_half, b, a).astype(BF16)

    q = q_ref[...].astype(F32)
    q = q * lax.rsqrt(_head_sumsq(q) * (1.0 / HEAD_DIM) + EPS) * qw_ref[...]
    lo_q = (lax.broadcasted_iota(jnp.int32, (1, ATTN_WIDTH), 1) & HALF) == 0
    qs_ref[0] = jnp.where(lo_q, q, 0.0).astype(BF16)
    qs_ref[1] = jnp.where(lo_q, 0.0, q).astype(BF16)

    r_idx = lax.broadcasted_iota(jnp.int32, (BLK, 2 * BLK), 0)
    c_idx = lax.broadcasted_iota(jnp.int32, (BLK, 2 * BLK), 1)
    band = (c_idx > r_idx) & (c_idx <= r_idx + BLK)
    row_f = lax.broadcasted_iota(jnp.int32, (BLK, 1), 0).astype(F32)
    col_f = lax.broadcasted_iota(jnp.int32, (1, 2 * BLK), 1).astype(F32)

    def block(t, carry):
        r0 = pl.multiple_of(t * BLK, BLK)
        first = ((step * nblk + t) % blocks_per_seq) == 0
        valid = band & (c_idx >= jnp.where(first, BLK, 0))
        for j in range(N_KV_HEADS):
            c0 = j * Q_PER_KV * HEAD_DIM
            lhs = jnp.concatenate(
                [qs_ref[half, pl.ds(r0, BLK), c0 + grp * LANES:c0 + (grp + 1) * LANES]
                 for grp in range(2) for half in range(2)], axis=0)
            kj = kd_ref[j, pl.ds(r0, 2 * BLK), :]
            vj = vd_ref[j, pl.ds(r0, 2 * BLK), :]
            s = _dot_nt(lhs, kj)
            probs, inv_den = [], []
            for hh in range(Q_PER_KV):
                h = j * Q_PER_KV + hh
                slope = ALIBI_SLOPES[h]
                sh = s[hh * BLK:(hh + 1) * BLK] + slope * col_f
                sh = jnp.where(valid, sh, -jnp.inf)
                sink = sink_ref[h] + slope * (row_f + float(BLK))
                mx = jnp.maximum(jnp.max(sh, axis=-1, keepdims=True), sink)
                p = jnp.exp(sh - mx)
                den = jnp.sum(p, axis=-1, keepdims=True) + jnp.exp(sink - mx)
                probs.append(p.astype(BF16))
                inv_den.append(1.0 / den)
            o = _dot(jnp.concatenate(probs, axis=0), vj)
            for grp in range(2):
                oa = o[(2 * grp) * BLK:(2 * grp + 1) * BLK] * inv_den[2 * grp]
                ob = o[(2 * grp + 1) * BLK:(2 * grp + 2) * BLK] * inv_den[2 * grp + 1]
                o_ref[pl.ds(r0, BLK), c0 + grp * LANES:c0 + (grp + 1) * LANES] = (
                    jnp.where(lo_half, oa, ob).astype(o_ref.dtype))
        return carry

    lax.fori_loop(0, nblk, block, 0)


def _attention(proj, q_norm_w, k_norm_w, sinks, *, seq):
    m = proj.shape[0]
    rows = min(MIX_ROWS, seq)
    nblk = rows // BLK
    qw = (jnp.tile(q_norm_w.astype(F32), N_Q_HEADS) * (HEAD_DIM ** -0.5))[None]
    kw = jnp.tile(k_norm_w.astype(F32), N_KV_HEADS)[None]
    kb, vb = K_OFF // KV_WIDTH, V_OFF // KV_WIDTH
    prev = lambda i: jnp.maximum(i * nblk - 1, 0)
    return pl.pallas_call(
        functools.partial(_attn_kernel, rows=rows, blocks_per_seq=seq // BLK),
        grid=(m // rows,),
        in_specs=[pl.BlockSpec(memory_space=pltpu.SMEM),
                  pl.BlockSpec((rows, ATTN_WIDTH), lambda i: (i, 0)),
                  pl.BlockSpec((rows, KV_WIDTH), lambda i: (i, kb)),
                  pl.BlockSpec((rows, KV_WIDTH), lambda i: (i, vb)),
                  pl.BlockSpec((BLK, KV_WIDTH), lambda i: (prev(i), kb)),
                  pl.BlockSpec((BLK, KV_WIDTH), lambda i: (prev(i), vb)),
                  pl.BlockSpec((1, ATTN_WIDTH), lambda i: (0, 0)),
                  pl.BlockSpec((1, KV_WIDTH), lambda i: (0, 0))],
        out_specs=pl.BlockSpec((rows, ATTN_WIDTH), lambda i: (i, 0)),
        out_shape=jax.ShapeDtypeStruct((m, ATTN_WIDTH), BF16),
        scratch_shapes=[pltpu.VMEM((2, rows, ATTN_WIDTH), BF16),
                        pltpu.VMEM((N_KV_HEADS, rows + BLK, LANES), BF16),
                        pltpu.VMEM((N_KV_HEADS, rows + BLK, LANES), BF16)],
        compiler_params=_params(1),
        name="swa_attention",
    )(sinks.astype(F32), proj, proj, proj, proj, proj, qw, kw)


def _sigmoid(x):
    return 1.0 / (1.0 + jnp.exp(-x))


def _ssd_kernel(z_ref, xbc_ref, dt_ref, cw_ref, cb_ref, dtb_ref, alog_ref, dskip_ref, nw_ref, o_ref,
                xwin_ref, u_ref, h_ref, *, rows):
    nchunk = rows // BLK
    first_step = pl.program_id(1) == 0

    @pl.when(first_step)
    def _():
        xwin_ref[0:BLK, :] = jnp.zeros((BLK, CONV_CH), BF16)
        h_ref[...] = jnp.zeros_like(h_ref)

    @pl.when(jnp.logical_not(first_step))
    def _():
        xwin_ref[0:BLK, :] = xwin_ref[rows:rows + BLK, :]

    xwin_ref[BLK:BLK + rows, :] = xbc_ref[...]

    out_t = lax.broadcasted_iota(jnp.int32, (BLK, 2 * BLK), 0)
    in_t = lax.broadcasted_iota(jnp.int32, (BLK, 2 * BLK), 1)
    shifts = [(in_t == out_t + (BLK - (CONV_WIDTH - 1 - k))).astype(BF16) for k in range(CONV_WIDTH)]

    lane = lax.broadcasted_iota(jnp.int32, (1, LANES), 1)
    lo_half = lane < HALF
    r_idx = lax.broadcasted_iota(jnp.int32, (BLK, BLK), 0)
    c_idx = lax.broadcasted_iota(jnp.int32, (BLK, BLK), 1)
    causal = c_idx <= r_idx
    tri = causal.astype(BF16)
    expand = (lax.broadcasted_iota(jnp.int32, (LANES, SSM_WIDTH), 0)
              == lax.broadcasted_iota(jnp.int32, (LANES, SSM_WIDTH), 1) // HEAD_DIM).astype(BF16)
    a_neg = jnp.where(lane < SSM_HEADS, -jnp.exp(alog_ref[...]), 0.0)

    def chunk(t, carry):
        r0 = pl.multiple_of(t * BLK, BLK)
        rws = pl.ds(r0, BLK)

        for c0 in range(0, CONV_CH, CONV_SLAB):
            cs = slice(c0, c0 + CONV_SLAB)
            win = xwin_ref[pl.ds(r0, 2 * BLK), cs]
            acc = cb_ref[:, cs]
            for k in range(CONV_WIDTH):
                acc = acc + cw_ref[k:k + 1, cs] * _dot(shifts[k], win)
            u_ref[:, cs] = acc * _sigmoid(acc)

        v = dt_ref[rws, :] + dtb_ref[...]
        dt = jnp.maximum(v, 0.0) + jnp.log1p(jnp.exp(-jnp.abs(v)))
        da = dt * a_neg
        cum = sum(_dot(tri, part) for part in _split_bf16(da, 3))
        cum_t = cum.T[0:SSM_HEADS]
        dt_t = dt.T[0:SSM_HEADS]
        w_t = jnp.exp(cum_t[:, BLK - 1:BLK] - cum_t) * dt_t
        p_hi, p_lo = _split_bf16(jnp.exp(cum), 2)
        p_exp = _dot(p_hi, expand) + _dot(p_lo, expand)

        for g in range(SSM_GROUPS):
            b_g = u_ref[:, SSM_WIDTH + g * D_STATE:SSM_WIDTH + (g + 1) * D_STATE]
            c_g = u_ref[:, SSM_WIDTH + (SSM_GROUPS + g) * D_STATE:SSM_WIDTH + (SSM_GROUPS + g + 1) * D_STATE]
            c_bf = c_g.astype(BF16)
            cb = _dot_nt(c_bf, b_g.astype(BF16))
            b_t = b_g.T
            gated = []
            for pair in range(2 * g, 2 * g + 2):
                cols = slice(pair * LANES, (pair + 1) * LANES)
                x_f = u_ref[:, cols]
                x_bf = x_f.astype(BF16)
                y_k, s_k = [], []
                for k in (2 * pair, 2 * pair + 1):
                    seg = cum[:, k:k + 1] - cum_t[k:k + 1, :]
                    decay = jnp.exp(jnp.where(causal, seg, -jnp.inf))
                    y_k.append(_dot((cb * decay * dt_t[k:k + 1, :]).astype(BF16), x_bf))
                    s_k.append(_dot((b_t * w_t[k:k + 1, :]).astype(BF16), x_bf))
                h_prev = h_ref[pair]
                y = (jnp.where(lo_half, y_k[0], y_k[1])
                     + _dot(c_bf, h_prev.astype(BF16)) * p_exp[:, cols]
                     + dskip_ref[:, cols] * x_f)
                h_ref[pair] = h_prev * p_exp[BLK - 1:BLK, cols] + jnp.where(lo_half, s_k[0], s_k[1])
                zf = z_ref[rws, cols].astype(F32)
                gated.append(y * (zf * _sigmoid(zf)))
            gg = jnp.concatenate(gated, axis=1)
            ms = jnp.mean(gg * gg, axis=-1, keepdims=True)
            gcols = slice(2 * g * LANES, (2 * g + 2) * LANES)
            o_ref[rws, gcols] = (gg * lax.rsqrt(ms + EPS) * nw_ref[:, gcols]).astype(o_ref.dtype)
        return carry

    lax.fori_loop(0, nchunk, chunk, 0)


def _ssd(proj, dt_raw, conv_w, conv_b, dt_bias, a_log, d_skip, norm_w, *, batch, seq):
    rows = min(MIX_ROWS, seq)
    steps = seq // rows
    pad_heads = lambda v: jnp.pad(v.astype(F32), (0, LANES - SSM_HEADS))[None]
    row = lambda b, c: b * steps + c
    return pl.pallas_call(
        functools.partial(_ssd_kernel, rows=rows),
        grid=(batch, steps),
        in_specs=[pl.BlockSpec((rows, SSM_WIDTH), lambda b, c: (row(b, c), Z_OFF // SSM_WIDTH)),
                  pl.BlockSpec((rows, CONV_CH), lambda b, c: (row(b, c), XBC_OFF // CONV_CH)),
                  pl.BlockSpec((rows, LANES), lambda b, c: (row(b, c), 0)),
                  pl.BlockSpec((CONV_WIDTH, CONV_CH), lambda b, c: (0, 0)),
                  pl.BlockSpec((1, CONV_CH), lambda b, c: (0, 0)),
                  pl.BlockSpec((1, LANES), lambda b, c: (0, 0)),
                  pl.BlockSpec((1, LANES), lambda b, c: (0, 0)),
                  pl.BlockSpec((1, SSM_WIDTH), lambda b, c: (0, 0)),
                  pl.BlockSpec((1, SSM_WIDTH), lambda b, c: (0, 0))],
        out_specs=pl.BlockSpec((rows, SSM_WIDTH), lambda b, c: (row(b, c), 0)),
        out_shape=jax.ShapeDtypeStruct((batch * seq, SSM_WIDTH), BF16),
        scratch_shapes=[pltpu.VMEM((rows + BLK, CONV_CH), BF16),
                        pltpu.VMEM((BLK, CONV_CH), F32),
                        pltpu.VMEM((SSM_HEADS // 2, D_STATE, LANES), F32)],
        compiler_params=_params(2),
        name="ssd_mixer",
    )(proj, proj, dt_raw, conv_w.astype(F32), conv_b.astype(F32)[None], pad_heads(dt_bias),
      pad_heads(a_log), jnp.repeat(d_skip.astype(F32), HEAD_DIM)[None], norm_w.astype(F32)[None])


def _outproj_kernel(x_ref, a_ref, s_ref, wa_ref, ws_ref, o_ref):
    o_ref[...] = x_ref[...] + _dot(a_ref[...], wa_ref[...]) + _dot(s_ref[...], ws_ref[...])


def _out_proj(x, attn, ssm, w_attn, w_ssm):
    m, d = x.shape
    tm = min(ROW_TILE, m)
    return pl.pallas_call(
        _outproj_kernel,
        grid=(m // tm,),
        in_specs=[pl.BlockSpec((tm, d), lambda i: (i, 0)),
                  pl.BlockSpec((tm, ATTN_WIDTH), lambda i: (i, 0)),
                  pl.BlockSpec((tm, SSM_WIDTH), lambda i: (i, 0)),
                  _resident((ATTN_WIDTH, d)),
                  _resident((SSM_WIDTH, d))],
        out_specs=pl.BlockSpec((tm, d), lambda i: (i, 0)),
        out_shape=jax.ShapeDtypeStruct((m, d), F32),
        compiler_params=_params(1),
        name="out_proj",
    )(x, attn, ssm, w_attn, w_ssm)


def _ffn_kernel(x_ref, nw_ref, wg_ref, wu_ref, wd_ref, o_ref, h_ref):
    @pl.when(pl.program_id(1) == 0)
    def _():
        x = x_ref[...]
        ms = jnp.mean(x * x, axis=-1, keepdims=True)
        h_ref[...] = (x * lax.rsqrt(ms + EPS) * nw_ref[...]).astype(BF16)
        o_ref[...] = x

    h = h_ref[...]
    gate = _dot(h, wg_ref[...])
    up = _dot(h, wu_ref[...])
    act = (gate * _sigmoid(gate) * up).astype(BF16)
    o_ref[...] += _dot(act, wd_ref[...])


def _ffn(x, norm_w, w_gate, w_up, w_down):
    m, d = x.shape
    f = w_gate.shape[1]
    tm, tf = min(ROW_TILE, m), FF_TILE
    return pl.pallas_call(
        _ffn_kernel,
        grid=(m // tm, f // tf),
        in_specs=[pl.BlockSpec((tm, d), lambda i, j: (i, 0)),
                  pl.BlockSpec((1, d), lambda i, j: (0, 0)),
                  pl.BlockSpec((d, tf), lambda i, j: (0, j)),
                  pl.BlockSpec((d, tf), lambda i, j: (0, j)),
                  pl.BlockSpec((tf, d), lambda i, j: (j, 0))],
        out_specs=pl.BlockSpec((tm, d), lambda i, j: (i, 0)),
        out_shape=jax.ShapeDtypeStruct((m, d), F32),
        scratch_shapes=[pltpu.VMEM((tm, d), BF16)],
        compiler_params=_params(2),
        name="swiglu_ffn",
    )(x, norm_w, w_gate, w_up, w_down)


def _layer(x, p, *, batch, seq):
    (attn_norm_w, w_in, q_norm_w, k_norm_w, sinks, conv_w, conv_b, dt_bias, a_log, d_skip,
     ssm_norm_w, w_out, ffn_norm_w, w_gate, w_up, w_down) = p
    q0, k0, v0, z0, x0, t0 = np.cumsum((0, ATTN_WIDTH, KV_WIDTH, KV_WIDTH, SSM_WIDTH, CONV_CH))
    w_main = jnp.concatenate([w_in[:, q0:k0], w_in[:, z0:x0], w_in[:, x0:t0], w_in[:, k0:v0], w_in[:, v0:z0]],
                             axis=1).astype(BF16)
    w_dt = jnp.pad(w_in[:, t0:], ((0, 0), (0, LANES - SSM_HEADS))).astype(BF16)

    proj, dt_raw = _norm_inproj(x, attn_norm_w.astype(F32)[None], w_main, w_dt)
    attn = _attention(proj, q_norm_w, k_norm_w, sinks, seq=seq)
    ssm = _ssd(proj, dt_raw, conv_w, conv_b, dt_bias, a_log, d_skip, ssm_norm_w, batch=batch, seq=seq)
    x = _out_proj(x, attn, ssm, w_out[:ATTN_WIDTH].astype(BF16), w_out[ATTN_WIDTH:].astype(BF16))
    return _ffn(x, ffn_norm_w.astype(F32)[None], w_gate.astype(BF16), w_up.astype(BF16), w_down.astype(BF16))


def kernel(x, attn_norm_w, w_in, q_norm_w, k_norm_w, sinks, conv_w, conv_b, dt_bias, a_log, d_skip,
           ssm_norm_w, w_out, ffn_norm_w, w_gate, w_up, w_down):
    batch, seq, d = x.shape
    stacked = (attn_norm_w, w_in, q_norm_w, k_norm_w, sinks, conv_w, conv_b, dt_bias, a_log, d_skip,
               ssm_norm_w, w_out, ffn_norm_w, w_gate, w_up, w_down)
    h = x.reshape(batch * seq, d)
    for layer in range(attn_norm_w.shape[0]):
        h = _layer(h, tuple(t[layer] for t in stacked), batch=batch, seq=seq)
    return h.reshape(batch, seq, d)
```

```python
import functools

import numpy as np
import jax
import jax.numpy as jnp
from jax import lax
from jax.experimental import pallas as pl
from jax.experimental.pallas import tpu as pltpu

F32 = jnp.float32
BF16 = jnp.bfloat16

D_MODEL = 2048
HEAD_DIM = 64
ATTN_WIDTH = 1024
N_Q_HEADS = ATTN_WIDTH // HEAD_DIM
N_KV_HEADS = 4
Q_PER_KV = N_Q_HEADS // N_KV_HEADS
KV_WIDTH = N_KV_HEADS * HEAD_DIM
BLK = 128
SSM_WIDTH = 1024
SSM_HEADS = SSM_WIDTH // HEAD_DIM
SSM_GROUPS = 4
D_STATE = 128
CONV_WIDTH = 4
CONV_CH = SSM_WIDTH + 2 * SSM_GROUPS * D_STATE
D_FF = 5632
EPS = 1e-6

LANES = 128
SUBLANES = 8
HALF = LANES // 2
MXU_DIM = 256
VMEM_LIMIT = 60 * 1024 * 1024

K_OFF = ATTN_WIDTH
V_OFF = K_OFF + KV_WIDTH
Z_OFF = V_OFF + KV_WIDTH
XBC_OFF = Z_OFF + SSM_WIDTH
PROJ_WIDTH = XBC_OFF + CONV_CH
SSD_COL_BLOCK = 512

ROW_TILE = 512
INPROJ_COL_TILE = 1536
OUT_SLAB = 512
FFN_ROW_TILE = 1024
FF_TILE = 512
MIX_ROWS = 512
CONV_SLAB = 512

ALIBI_SLOPES = tuple(float(s) for s in np.power(
    np.float32(2.0), -8.0 * np.arange(1, N_Q_HEADS + 1, dtype=np.float32) / N_Q_HEADS))


def _params(n_axes):
    return pltpu.CompilerParams(dimension_semantics=("arbitrary",) * n_axes,
                                vmem_limit_bytes=VMEM_LIMIT)


def _split_bf16(x, parts):
    out = []
    for _ in range(parts - 1):
        hi = x.astype(BF16)
        out.append(hi)
        x = x - hi.astype(F32)
    out.append(x.astype(BF16))
    return out


def _dot(a, b):
    return jnp.dot(a, b, preferred_element_type=F32)


def _dot_nt(a, b):
    return lax.dot_general(a, b, (((1,), (1,)), ((), ())), preferred_element_type=F32)


def _norm_inproj_kernel(x_ref, nw_ref, w_ref, wdt_ref, o_ref, odt_ref, h_ref):
    x = x_ref[...]
    ms = jnp.mean(x * x, axis=-1, keepdims=True)
    h_ref[...] = (x * lax.rsqrt(ms + EPS) * nw_ref[...]).astype(BF16)
    odt_ref[...] = _dot(h_ref[...], wdt_ref[...])
    for n in range(o_ref.shape[1] // INPROJ_COL_TILE):
        cols = slice(n * INPROJ_COL_TILE, (n + 1) * INPROJ_COL_TILE)
        o_ref[:, cols] = _dot(h_ref[...], w_ref[:, cols]).astype(o_ref.dtype)


def _resident(shape):
    return pl.BlockSpec(shape, lambda i: (0,) * len(shape), pipeline_mode=pl.Buffered(1))


def _norm_inproj(x, norm_w, w_main, w_dt):
    m, d = x.shape
    n = w_main.shape[1]
    tm = min(ROW_TILE, m)
    return pl.pallas_call(
        _norm_inproj_kernel,
        grid=(m // tm,),
        in_specs=[pl.BlockSpec((tm, d), lambda i: (i, 0)),
                  _resident((1, d)),
                  _resident((d, n)),
                  _resident((d, LANES))],
        out_specs=[pl.BlockSpec((tm, n), lambda i: (i, 0)),
                   pl.BlockSpec((tm, LANES), lambda i: (i, 0))],
        out_shape=[jax.ShapeDtypeStruct((m, n), BF16),
                   jax.ShapeDtypeStruct((m, LANES), F32)],
        scratch_shapes=[pltpu.VMEM((tm, d), BF16)],
        compiler_params=_params(1),
        name="norm_inproj",
    )(x, norm_w, w_main, w_dt)


def _head_sumsq(x):
    r = lax.broadcasted_iota(jnp.int32, (MXU_DIM, MXU_DIM), 0) // HEAD_DIM
    c = lax.broadcasted_iota(jnp.int32, (MXU_DIM, MXU_DIM), 1) // HEAD_DIM
    seg = (r == c).astype(BF16)
    outs = []
    for g in range(x.shape[1] // MXU_DIM):
        xg = x[:, g * MXU_DIM:(g + 1) * MXU_DIM]
        hi, lo = _split_bf16(xg * xg, 2)
        outs.append(_dot(hi, seg) + _dot(lo, seg))
    return outs[0] if len(outs) == 1 else jnp.concatenate(outs, axis=1)


def _attn_kernel(sink_ref, q_ref, kc_ref, vc_ref, kp_ref, vp_ref, qw_ref, kw_ref, o_ref,
                 qs_ref, kd_ref, vd_ref, *, rows, blocks_per_seq):
    step = pl.program_id(0)
    nblk = rows // BLK
    lo_half = lax.broadcasted_iota(jnp.int32, (1, LANES), 1) < HALF

    k_all = jnp.concatenate([kp_ref[...], kc_ref[...]], axis=0).astype(F32)
    v_all = jnp.concatenate([vp_ref[...], vc_ref[...]], axis=0).astype(F32)
    k_all = k_all * lax.rsqrt(_head_sumsq(k_all) * (1.0 / HEAD_DIM) + EPS) * kw_ref[...]
    for src, dst in ((k_all, kd_ref), (v_all, vd_ref)):
        for p in range(KV_WIDTH // LANES):
            a = src[:, p * LANES:(p + 1) * LANES]
            b = pltpu.roll(a, HALF, 1)
            dst[2 * p] = jnp.where(lo_half, a, b).astype(BF16)
            dst[2 * p + 1] = jnp.where(lo_half, b, a).astype(BF16)

    q = q_ref[...].astype(F32)
    q = q * lax.rsqrt(_head_sumsq(q) * (1.0 / HEAD_DIM) + EPS) * qw_ref[...]
    lo_q = (lax.broadcasted_iota(jnp.int32, (1, ATTN_WIDTH), 1) & HALF) == 0
    qs_ref[0] = jnp.where(lo_q, q, 0.0).astype(BF16)
    qs_ref[1] = jnp.where(lo_q, 0.0, q).astype(BF16)

    r_idx = lax.broadcasted_iota(jnp.int32, (BLK, 2 * BLK), 0)
    c_idx = lax.broadcasted_iota(jnp.int32, (BLK, 2 * BLK), 1)
    band = (c_idx > r_idx) & (c_idx <= r_idx + BLK)
    row_f = lax.broadcasted_iota(jnp.int32, (BLK, 1), 0).astype(F32)
    col_f = lax.broadcasted_iota(jnp.int32, (1, 2 * BLK), 1).astype(F32)

    def block(t, carry):
        r0 = pl.multiple_of(t * BLK, BLK)
        first = ((step * nblk + t) % blocks_per_seq) == 0
        valid = band & (c_idx >= jnp.where(first, BLK, 0))
        for j in range(N_KV_HEADS):
            c0 = j * Q_PER_KV * HEAD_DIM
            lhs = jnp.concatenate(
                [qs_ref[half, pl.ds(r0, BLK), c0 + grp * LANES:c0 + (grp + 1) * LANES]
                 for grp in range(2) for half in range(2)], axis=0)
            kj = kd_ref[j, pl.ds(r0, 2 * BLK), :]
            vj = vd_ref[j, pl.ds(r0, 2 * BLK), :]
            s = _dot_nt(lhs, kj)
            probs, inv_den = [], []
            for hh in range(Q_PER_KV):
                h = j * Q_PER_KV + hh
                slope = ALIBI_SLOPES[h]
                sh = s[hh * BLK:(hh + 1) * BLK] + slope * col_f
                sh = jnp.where(valid, sh, -jnp.inf)
                sink = sink_ref[h] + slope * (row_f + float(BLK))
                mx = jnp.maximum(jnp.max(sh, axis=-1, keepdims=True), sink)
                p = jnp.exp(sh - mx)
                den = jnp.sum(p, axis=-1, keepdims=True) + jnp.exp(sink - mx)
                probs.append(p.astype(BF16))
                inv_den.append(1.0 / den)
            o = _dot(jnp.concatenate(probs, axis=0), vj)
            for grp in range(2):
                oa = o[(2 * grp) * BLK:(2 * grp + 1) * BLK] * inv_den[2 * grp]
                ob = o[(2 * grp + 1) * BLK:(2 * grp + 2) * BLK] * inv_den[2 * grp + 1]
                o_ref[pl.ds(r0, BLK), c0 + grp * LANES:c0 + (grp + 1) * LANES] = (
                    jnp.where(lo_half, oa, ob).astype(o_ref.dtype))
        return carry

    lax.fori_loop(0, nblk, block, 0, unroll=2)


def _attention(proj, q_norm_w, k_norm_w, sinks, *, seq):
    m = proj.shape[0]
    rows = min(MIX_ROWS, seq)
    nblk = rows // BLK
    qw = (jnp.tile(q_norm_w.astype(F32), N_Q_HEADS) * (HEAD_DIM ** -0.5))[None]
    kw = jnp.tile(k_norm_w.astype(F32), N_KV_HEADS)[None]
    kb, vb = K_OFF // KV_WIDTH, V_OFF // KV_WIDTH
    prev = lambda i: jnp.maximum(i * nblk - 1, 0)
    return pl.pallas_call(
        functools.partial(_attn_kernel, rows=rows, blocks_per_seq=seq // BLK),
        grid=(m // rows,),
        in_specs=[pl.BlockSpec(memory_space=pltpu.SMEM),
                  pl.BlockSpec((rows, ATTN_WIDTH), lambda i: (i, 0)),
                  pl.BlockSpec((rows, KV_WIDTH), lambda i: (i, kb)),
                  pl.BlockSpec((rows, KV_WIDTH), lambda i: (i, vb)),
                  pl.BlockSpec((BLK, KV_WIDTH), lambda i: (prev(i), kb)),
                  pl.BlockSpec((BLK, KV_WIDTH), lambda i: (prev(i), vb)),
                  pl.BlockSpec((1, ATTN_WIDTH), lambda i: (0, 0)),
                  pl.BlockSpec((1, KV_WIDTH), lambda i: (0, 0))],
        out_specs=pl.BlockSpec((rows, ATTN_WIDTH), lambda i: (i, 0)),
        out_shape=jax.ShapeDtypeStruct((m, ATTN_WIDTH), BF16),
        scratch_shapes=[pltpu.VMEM((2, rows, ATTN_WIDTH), BF16),
                        pltpu.VMEM((N_KV_HEADS, rows + BLK, LANES), BF16),
                        pltpu.VMEM((N_KV_HEADS, rows + BLK, LANES), BF16)],
        compiler_params=_params(1),
        name="swa_attention",
    )(sinks.astype(F32), proj, proj, proj, proj, proj, qw, kw)


def _sigmoid(x):
    return 1.0 / (1.0 + jnp.exp(-x))


def _ssd_kernel(*refs, rows):
    nz, nx = SSM_WIDTH // SSD_COL_BLOCK, CONV_CH // SSD_COL_BLOCK
    z_refs, xbc_refs = refs[:nz], refs[nz:nz + nx]
    (dt_ref, cw_ref, cb_ref, dtb_ref, alog_ref, dskip_ref, nw_ref, o_ref,
     xwin_ref, u_ref, h_ref) = refs[nz + nx:]
    nchunk = rows // BLK
    first_step = pl.program_id(1) == 0

    @pl.when(first_step)
    def _():
        xwin_ref[0:BLK, :] = jnp.zeros((BLK, CONV_CH), BF16)
        h_ref[...] = jnp.zeros_like(h_ref)

    @pl.when(jnp.logical_not(first_step))
    def _():
        xwin_ref[0:BLK, :] = xwin_ref[rows:rows + BLK, :]

    for n, ref in enumerate(xbc_refs):
        xwin_ref[BLK:BLK + rows, n * SSD_COL_BLOCK:(n + 1) * SSD_COL_BLOCK] = ref[...]

    out_t = lax.broadcasted_iota(jnp.int32, (BLK, 2 * BLK), 0)
    in_t = lax.broadcasted_iota(jnp.int32, (BLK, 2 * BLK), 1)
    shifts = [(in_t == out_t + (BLK - (CONV_WIDTH - 1 - k))).astype(BF16) for k in range(CONV_WIDTH)]

    lane = lax.broadcasted_iota(jnp.int32, (1, LANES), 1)
    lo_half = lane < HALF
    r_idx = lax.broadcasted_iota(jnp.int32, (BLK, BLK), 0)
    c_idx = lax.broadcasted_iota(jnp.int32, (BLK, BLK), 1)
    causal = c_idx <= r_idx
    tri = causal.astype(BF16)
    expand = (lax.broadcasted_iota(jnp.int32, (LANES, SSM_WIDTH), 0)
              == lax.broadcasted_iota(jnp.int32, (LANES, SSM_WIDTH), 1) // HEAD_DIM).astype(BF16)
    a_neg = jnp.where(lane < SSM_HEADS, -jnp.exp(alog_ref[...]), 0.0)

    def chunk(t, carry):
        r0 = pl.multiple_of(t * BLK, BLK)
        rws = pl.ds(r0, BLK)

        for c0 in range(0, CONV_CH, CONV_SLAB):
            cs = slice(c0, c0 + CONV_SLAB)
            win = xwin_ref[pl.ds(r0, 2 * BLK), cs]
            acc = cb_ref[:, cs]
            for k in range(CONV_WIDTH):
                acc = acc + cw_ref[k:k + 1, cs] * _dot(shifts[k], win)
            u_ref[:, cs] = acc * _sigmoid(acc)

        v = dt_ref[rws, :] + dtb_ref[...]
        dt = jnp.maximum(v, 0.0) + jnp.log1p(jnp.exp(-jnp.abs(v)))
        da = dt * a_neg
        cum = sum(_dot(tri, part) for part in _split_bf16(da, 3))
        cum_t = cum.T[0:SSM_HEADS]
        dt_t = dt.T[0:SSM_HEADS]
        w_t = jnp.exp(cum_t[:, BLK - 1:BLK] - cum_t) * dt_t
        p_hi, p_lo = _split_bf16(jnp.exp(cum), 2)
        p_exp = _dot(p_hi, expand) + _dot(p_lo, expand)

        for g in range(SSM_GROUPS):
            b_g = u_ref[:, SSM_WIDTH + g * D_STATE:SSM_WIDTH + (g + 1) * D_STATE]
            c_g = u_ref[:, SSM_WIDTH + (SSM_GROUPS + g) * D_STATE:SSM_WIDTH + (SSM_GROUPS + g + 1) * D_STATE]
            c_bf = c_g.astype(BF16)
            cb = _dot_nt(c_bf, b_g.astype(BF16))
            b_t = b_g.T
            gated = []
            for pair in range(2 * g, 2 * g + 2):
                cols = slice(pair * LANES, (pair + 1) * LANES)
                x_f = u_ref[:, cols]
                x_bf = x_f.astype(BF16)
                y_k, s_k = [], []
                for k in (2 * pair, 2 * pair + 1):
                    seg = cum[:, k:k + 1] - cum_t[k:k + 1, :]
                    decay = jnp.exp(jnp.where(causal, seg, -jnp.inf))
                    y_k.append(_dot((cb * decay * dt_t[k:k + 1, :]).astype(BF16), x_bf))
                    s_k.append(_dot((b_t * w_t[k:k + 1, :]).astype(BF16), x_bf))
                h_prev = h_ref[pair]
                y = (jnp.where(lo_half, y_k[0], y_k[1])
                     + _dot(c_bf, h_prev.astype(BF16)) * p_exp[:, cols]
                     + dskip_ref[:, cols] * x_f)
                h_ref[pair] = h_prev * p_exp[BLK - 1:BLK, cols] + jnp.where(lo_half, s_k[0], s_k[1])
                zb, zc = divmod(pair * LANES, SSD_COL_BLOCK)
                zf = z_refs[zb][rws, zc:zc + LANES].astype(F32)
                gated.append(y * (zf * _sigmoid(zf)))
            gg = jnp.concatenate(gated, axis=1)
            ms = jnp.mean(gg * gg, axis=-1, keepdims=True)
            gcols = slice(2 * g * LANES, (2 * g + 2) * LANES)
            o_ref[rws, gcols] = (gg * lax.rsqrt(ms + EPS) * nw_ref[:, gcols]).astype(o_ref.dtype)
        return carry

    lax.fori_loop(0, nchunk, chunk, 0, unroll=2)


def _ssd(proj, dt_raw, conv_w, conv_b, dt_bias, a_log, d_skip, norm_w, *, batch, seq):
    rows = min(MIX_ROWS, seq)
    steps = seq // rows
    pad_heads = lambda v: jnp.pad(v.astype(F32), (0, LANES - SSM_HEADS))[None]
    row = lambda b, c: b * steps + c
    col_blocks = range(Z_OFF // SSD_COL_BLOCK, PROJ_WIDTH // SSD_COL_BLOCK)
    proj_specs = [pl.BlockSpec((rows, SSD_COL_BLOCK), lambda b, c, n=n: (row(b, c), n)) for n in col_blocks]
    return pl.pallas_call(
        functools.partial(_ssd_kernel, rows=rows),
        grid=(batch, steps),
        in_specs=proj_specs + [
                  pl.BlockSpec((rows, LANES), lambda b, c: (row(b, c), 0)),
                  pl.BlockSpec((CONV_WIDTH, CONV_CH), lambda b, c: (0, 0)),
                  pl.BlockSpec((1, CONV_CH), lambda b, c: (0, 0)),
                  pl.BlockSpec((1, LANES), lambda b, c: (0, 0)),
                  pl.BlockSpec((1, LANES), lambda b, c: (0, 0)),
                  pl.BlockSpec((1, SSM_WIDTH), lambda b, c: (0, 0)),
                  pl.BlockSpec((1, SSM_WIDTH), lambda b, c: (0, 0))],
        out_specs=pl.BlockSpec((rows, SSM_WIDTH), lambda b, c: (row(b, c), 0)),
        out_shape=jax.ShapeDtypeStruct((batch * seq, SSM_WIDTH), BF16),
        scratch_shapes=[pltpu.VMEM((rows + BLK, CONV_CH), BF16),
                        pltpu.VMEM((BLK, CONV_CH), F32),
                        pltpu.VMEM((SSM_HEADS // 2, D_STATE, LANES), F32)],
        compiler_params=_params(2),
        name="ssd_mixer",
    )(*([proj] * len(proj_specs)), dt_raw, conv_w.astype(F32), conv_b.astype(F32)[None], pad_heads(dt_bias),
      pad_heads(a_log), jnp.repeat(d_skip.astype(F32), HEAD_DIM)[None], norm_w.astype(F32)[None])


def _outproj_kernel(x_ref, a_ref, s_ref, wa_ref, ws_ref, nw_ref, o_ref, hw_ref, r_ref):
    d = o_ref.shape[1]
    ss = jnp.zeros((o_ref.shape[0], 1), F32)
    for c0 in range(0, d, OUT_SLAB):
        cols = slice(c0, c0 + OUT_SLAB)
        x1 = x_ref[:, cols] + _dot(a_ref[...], wa_ref[:, cols]) + _dot(s_ref[...], ws_ref[:, cols])
        o_ref[:, cols] = x1
        hw_ref[:, cols] = (x1 * nw_ref[:, cols]).astype(BF16)
        ss = ss + jnp.sum(x1 * x1, axis=-1, keepdims=True)
    r_ref[...] = jnp.broadcast_to(lax.rsqrt(ss * (1.0 / d) + EPS), r_ref.shape)


def _out_proj(x, attn, ssm, w_attn, w_ssm, ffn_norm_w):
    m, d = x.shape
    tm = min(ROW_TILE, m)
    rows = lambda width: pl.BlockSpec((tm, width), lambda i: (i, 0))
    return pl.pallas_call(
        _outproj_kernel,
        grid=(m // tm,),
        in_specs=[rows(d), rows(ATTN_WIDTH), rows(SSM_WIDTH),
                  _resident((ATTN_WIDTH, d)), _resident((SSM_WIDTH, d)), _resident((1, d))],
        out_specs=[rows(d), rows(d), rows(LANES)],
        out_shape=[jax.ShapeDtypeStruct((m, d), F32),
                   jax.ShapeDtypeStruct((m, d), BF16),
                   jax.ShapeDtypeStruct((m, LANES), F32)],
        compiler_params=_params(1),
        name="out_proj",
    )(x, attn, ssm, w_attn, w_ssm, ffn_norm_w)


def _ffn_kernel(x_ref, hw_ref, r_ref, wg_ref, wu_ref, wd_ref, o_ref):
    def tile(first):
        hw = hw_ref[...]
        r = r_ref[...]
        gate = _dot(hw, wg_ref[...])
        up = _dot(hw, wu_ref[...])
        act = jnp.concatenate(
            [((gate[:, c:c + LANES] * up[:, c:c + LANES]) * (r * r)
              * _sigmoid(gate[:, c:c + LANES] * r)).astype(BF16)
             for c in range(0, gate.shape[1], LANES)], axis=1)
        down = _dot(act, wd_ref[...])
        if first:
            o_ref[...] = x_ref[...] + down
        else:
            o_ref[...] += down

    first_tile = pl.program_id(1) == 0
    pl.when(first_tile)(functools.partial(tile, True))
    pl.when(jnp.logical_not(first_tile))(functools.partial(tile, False))


def _ffn(x, hw, r, w_gate, w_up, w_down):
    m, d = x.shape
    f = w_gate.shape[1]
    tm, tf = min(FFN_ROW_TILE, m), FF_TILE
    rows = lambda width: pl.BlockSpec((tm, width), lambda i, j: (i, 0))
    return pl.pallas_call(
        _ffn_kernel,
        grid=(m // tm, f // tf),
        in_specs=[rows(d), rows(d), rows(LANES),
                  pl.BlockSpec((d, tf), lambda i, j: (0, j)),
                  pl.BlockSpec((d, tf), lambda i, j: (0, j)),
                  pl.BlockSpec((tf, d), lambda i, j: (j, 0))],
        out_specs=rows(d),
        out_shape=jax.ShapeDtypeStruct((m, d), F32),
        compiler_params=_params(2),
        name="swiglu_ffn",
    )(x, hw, r, w_gate, w_up, w_down)


def _layer(x, p, *, batch, seq):
    (attn_norm_w, w_in, q_norm_w, k_norm_w, sinks, conv_w, conv_b, dt_bias, a_log, d_skip,
     ssm_norm_w, w_out, ffn_norm_w, w_gate, w_up, w_down) = p
    w_main = w_in[:, :PROJ_WIDTH].astype(BF16)
    w_dt = jnp.pad(w_in[:, PROJ_WIDTH:], ((0, 0), (0, LANES - SSM_HEADS))).astype(BF16)

    proj, dt_raw = _norm_inproj(x, attn_norm_w.astype(F32)[None], w_main, w_dt)
    attn = _attention(proj, q_norm_w, k_norm_w, sinks, seq=seq)
    ssm = _ssd(proj, dt_raw, conv_w, conv_b, dt_bias, a_log, d_skip, ssm_norm_w, batch=batch, seq=seq)
    x, hw, r = _out_proj(x, attn, ssm, w_out[:ATTN_WIDTH].astype(BF16), w_out[ATTN_WIDTH:].astype(BF16),
                         ffn_norm_w.astype(F32)[None])
    return _ffn(x, hw, r, w_gate.astype(BF16), w_up.astype(BF16), w_down.astype(BF16))


def kernel(x, attn_norm_w, w_in, q_norm_w, k_norm_w, sinks, conv_w, conv_b, dt_bias, a_log, d_skip,
           ssm_norm_w, w_out, ffn_norm_w, w_gate, w_up, w_down):
    batch, seq, d = x.shape
    stacked = (attn_norm_w, w_in, q_norm_w, k_norm_w, sinks, conv_w, conv_b, dt_bias, a_log, d_skip,
               ssm_norm_w, w_out, ffn_norm_w, w_gate, w_up, w_down)
    h = x.reshape(batch * seq, d)
    for layer in range(attn_norm_w.shape[0]):
        h = _layer(h, tuple(t[layer] for t in stacked), batch=batch, seq=seq)
    return h.reshape(batch, seq, d)
```

```python
import functools

import numpy as np
import jax
import jax.numpy as jnp
from jax import lax
from jax.experimental import pallas as pl
from jax.experimental.pallas import tpu as pltpu

F32 = jnp.float32
BF16 = jnp.bfloat16

D_MODEL = 2048
HEAD_DIM = 64
ATTN_WIDTH = 1024
N_Q_HEADS = ATTN_WIDTH // HEAD_DIM
N_KV_HEADS = 4
Q_PER_KV = N_Q_HEADS // N_KV_HEADS
KV_WIDTH = N_KV_HEADS * HEAD_DIM
BLK = 128
SSM_WIDTH = 1024
SSM_HEADS = SSM_WIDTH // HEAD_DIM
SSM_GROUPS = 4
D_STATE = 128
CONV_WIDTH = 4
CONV_CH = SSM_WIDTH + 2 * SSM_GROUPS * D_STATE
D_FF = 5632
EPS = 1e-6

LANES = 128
SUBLANES = 8
HALF = LANES // 2
MXU_DIM = 256
VMEM_LIMIT = 60 * 1024 * 1024

K_OFF = ATTN_WIDTH
V_OFF = K_OFF + KV_WIDTH
Z_OFF = V_OFF + KV_WIDTH
XBC_OFF = Z_OFF + SSM_WIDTH
PROJ_WIDTH = XBC_OFF + CONV_CH
SSD_COL_BLOCK = 512

ROW_TILE = 512
INPROJ_COL_TILE = 1536
OUT_SLAB = 512
FFN_ROW_TILE = 1024
FF_TILE = 512
MIX_ROWS = 512
CONV_SLAB = 512

ALIBI_SLOPES = tuple(float(s) for s in np.power(
    np.float32(2.0), -8.0 * np.arange(1, N_Q_HEADS + 1, dtype=np.float32) / N_Q_HEADS))


def _params(n_axes):
    return pltpu.CompilerParams(dimension_semantics=("arbitrary",) * n_axes,
                                vmem_limit_bytes=VMEM_LIMIT)


def _split_bf16(x, parts):
    out = []
    for _ in range(parts - 1):
        hi = x.astype(BF16)
        out.append(hi)
        x = x - hi.astype(F32)
    out.append(x.astype(BF16))
    return out


def _dot(a, b):
    return jnp.dot(a, b, preferred_element_type=F32)


def _dot_nt(a, b):
    return lax.dot_general(a, b, (((1,), (1,)), ((), ())), preferred_element_type=F32)


def _norm_inproj_kernel(x_ref, nw_ref, w_ref, wdt_ref, o_ref, odt_ref, h_ref):
    x = x_ref[...]
    h_ref[...] = (x * nw_ref[...]).astype(BF16)
    r = lax.rsqrt(jnp.mean(x * x, axis=-1, keepdims=True) + EPS)
    odt_ref[...] = _dot(h_ref[...], wdt_ref[...]) * r
    for n in range(o_ref.shape[1] // INPROJ_COL_TILE):
        cols = slice(n * INPROJ_COL_TILE, (n + 1) * INPROJ_COL_TILE)
        o_ref[:, cols] = (_dot(h_ref[...], w_ref[:, cols]) * r).astype(o_ref.dtype)


def _resident(shape):
    return pl.BlockSpec(shape, lambda i: (0,) * len(shape), pipeline_mode=pl.Buffered(1))


def _norm_inproj(x, norm_w, w_main, w_dt):
    m, d = x.shape
    n = w_main.shape[1]
    tm = min(ROW_TILE, m)
    return pl.pallas_call(
        _norm_inproj_kernel,
        grid=(m // tm,),
        in_specs=[pl.BlockSpec((tm, d), lambda i: (i, 0)),
                  _resident((1, d)),
                  _resident((d, n)),
                  _resident((d, LANES))],
        out_specs=[pl.BlockSpec((tm, n), lambda i: (i, 0)),
                   pl.BlockSpec((tm, LANES), lambda i: (i, 0))],
        out_shape=[jax.ShapeDtypeStruct((m, n), BF16),
                   jax.ShapeDtypeStruct((m, LANES), F32)],
        scratch_shapes=[pltpu.VMEM((tm, d), BF16)],
        compiler_params=_params(1),
        name="norm_inproj",
    )(x, norm_w, w_main, w_dt)


def _const(a):
    return jnp.asarray(np.asarray(a, np.float32), BF16)


def _const_spec(shape):
    return pl.BlockSpec(shape, lambda *_: (0,) * len(shape))


HEAD_SEGMENTS = np.arange(MXU_DIM)[:, None] // HEAD_DIM == np.arange(MXU_DIM)[None, :] // HEAD_DIM


def _head_sumsq(x, seg):
    outs = []
    for g in range(x.shape[1] // MXU_DIM):
        xg = x[:, g * MXU_DIM:(g + 1) * MXU_DIM]
        hi, lo = _split_bf16(xg * xg, 2)
        outs.append(_dot(hi, seg) + _dot(lo, seg))
    return outs[0] if len(outs) == 1 else jnp.concatenate(outs, axis=1)


def _attn_kernel(sink_ref, q_ref, kc_ref, vc_ref, kp_ref, vp_ref, qw_ref, kw_ref, seg_ref, o_ref,
                 qs_ref, kd_ref, vd_ref, *, rows, blocks_per_seq):
    step = pl.program_id(0)
    nblk = rows // BLK
    lo_half = lax.broadcasted_iota(jnp.int32, (1, LANES), 1) < HALF
    seg = seg_ref[...]

    k_all = jnp.concatenate([kp_ref[...], kc_ref[...]], axis=0).astype(F32)
    v_all = jnp.concatenate([vp_ref[...], vc_ref[...]], axis=0).astype(F32)
    k_all = k_all * lax.rsqrt(_head_sumsq(k_all, seg) * (1.0 / HEAD_DIM) + EPS) * kw_ref[...]
    for src, dst in ((k_all, kd_ref), (v_all, vd_ref)):
        for p in range(KV_WIDTH // LANES):
            a = src[:, p * LANES:(p + 1) * LANES]
            b = pltpu.roll(a, HALF, 1)
            dst[2 * p] = jnp.where(lo_half, a, b).astype(BF16)
            dst[2 * p + 1] = jnp.where(lo_half, b, a).astype(BF16)

    q = q_ref[...].astype(F32)
    q = q * lax.rsqrt(_head_sumsq(q, seg) * (1.0 / HEAD_DIM) + EPS) * qw_ref[...]
    lo_q = (lax.broadcasted_iota(jnp.int32, (1, ATTN_WIDTH), 1) & HALF) == 0
    qs_ref[0] = jnp.where(lo_q, q, 0.0).astype(BF16)
    qs_ref[1] = jnp.where(lo_q, 0.0, q).astype(BF16)

    r_idx = lax.broadcasted_iota(jnp.int32, (BLK, 2 * BLK), 0)
    c_idx = lax.broadcasted_iota(jnp.int32, (BLK, 2 * BLK), 1)
    band = (c_idx > r_idx) & (c_idx <= r_idx + BLK)
    row_f = lax.broadcasted_iota(jnp.int32, (BLK, 1), 0).astype(F32)
    col_f = lax.broadcasted_iota(jnp.int32, (1, 2 * BLK), 1).astype(F32)

    for t in range(nblk):
        r0 = t * BLK
        first = ((step * nblk + t) % blocks_per_seq) == 0
        valid = band & (c_idx >= jnp.where(first, BLK, 0))
        for j in range(N_KV_HEADS):
            c0 = j * Q_PER_KV * HEAD_DIM
            lhs = jnp.concatenate(
                [qs_ref[half, pl.ds(r0, BLK), c0 + grp * LANES:c0 + (grp + 1) * LANES]
                 for grp in range(2) for half in range(2)], axis=0)
            kj = kd_ref[j, pl.ds(r0, 2 * BLK), :]
            vj = vd_ref[j, pl.ds(r0, 2 * BLK), :]
            s = _dot_nt(lhs, kj)
            probs, inv_den = [], []
            for hh in range(Q_PER_KV):
                h = j * Q_PER_KV + hh
                slope = ALIBI_SLOPES[h]
                sh = s[hh * BLK:(hh + 1) * BLK] + slope * col_f
                sh = jnp.where(valid, sh, -jnp.inf)
                sink = sink_ref[h] + slope * (row_f + float(BLK))
                mx = jnp.maximum(jnp.max(sh, axis=-1, keepdims=True), sink)
                p = jnp.exp(sh - mx)
                den = jnp.sum(p, axis=-1, keepdims=True) + jnp.exp(sink - mx)
                probs.append(p.astype(BF16))
                inv_den.append(1.0 / den)
            o = _dot(jnp.concatenate(probs, axis=0), vj)
            for grp in range(2):
                oa = o[(2 * grp) * BLK:(2 * grp + 1) * BLK] * inv_den[2 * grp]
                ob = o[(2 * grp + 1) * BLK:(2 * grp + 2) * BLK] * inv_den[2 * grp + 1]
                o_ref[pl.ds(r0, BLK), c0 + grp * LANES:c0 + (grp + 1) * LANES] = (
                    jnp.where(lo_half, oa, ob).astype(o_ref.dtype))


def _attention(proj, q_norm_w, k_norm_w, sinks, *, seq):
    m = proj.shape[0]
    rows = min(MIX_ROWS, seq)
    nblk = rows // BLK
    qw = (jnp.tile(q_norm_w.astype(F32), N_Q_HEADS) * (HEAD_DIM ** -0.5))[None]
    kw = jnp.tile(k_norm_w.astype(F32), N_KV_HEADS)[None]
    kb, vb = K_OFF // KV_WIDTH, V_OFF // KV_WIDTH
    prev = lambda i: jnp.maximum(i * nblk - 1, 0)
    return pl.pallas_call(
        functools.partial(_attn_kernel, rows=rows, blocks_per_seq=seq // BLK),
        grid=(m // rows,),
        in_specs=[pl.BlockSpec(memory_space=pltpu.SMEM),
                  pl.BlockSpec((rows, ATTN_WIDTH), lambda i: (i, 0)),
                  pl.BlockSpec((rows, KV_WIDTH), lambda i: (i, kb)),
                  pl.BlockSpec((rows, KV_WIDTH), lambda i: (i, vb)),
                  pl.BlockSpec((BLK, KV_WIDTH), lambda i: (prev(i), kb)),
                  pl.BlockSpec((BLK, KV_WIDTH), lambda i: (prev(i), vb)),
                  pl.BlockSpec((1, ATTN_WIDTH), lambda i: (0, 0)),
                  pl.BlockSpec((1, KV_WIDTH), lambda i: (0, 0)),
                  _const_spec(HEAD_SEGMENTS.shape)],
        out_specs=pl.BlockSpec((rows, ATTN_WIDTH), lambda i: (i, 0)),
        out_shape=jax.ShapeDtypeStruct((m, ATTN_WIDTH), BF16),
        scratch_shapes=[pltpu.VMEM((2, rows, ATTN_WIDTH), BF16),
                        pltpu.VMEM((N_KV_HEADS, rows + BLK, LANES), BF16),
                        pltpu.VMEM((N_KV_HEADS, rows + BLK, LANES), BF16)],
        compiler_params=_params(1),
        name="swa_attention",
    )(sinks.astype(F32), proj, proj, proj, proj, proj, qw, kw, _const(HEAD_SEGMENTS))


def _sigmoid(x):
    return 1.0 / (1.0 + jnp.exp(-x))


def _ssd_kernel(*refs, rows):
    nz, nx = SSM_WIDTH // SSD_COL_BLOCK, CONV_CH // SSD_COL_BLOCK
    z_refs, xbc_refs = refs[:nz], refs[nz:nz + nx]
    (dt_ref, cw_ref, cb_ref, dtb_ref, alog_ref, dskip_ref, nw_ref, o_ref,
     xwin_ref, u_ref, h_ref) = refs[nz + nx:]
    nchunk = rows // BLK
    first_step = pl.program_id(1) == 0

    @pl.when(first_step)
    def _():
        xwin_ref[0:BLK, :] = jnp.zeros((BLK, CONV_CH), BF16)
        h_ref[...] = jnp.zeros_like(h_ref)

    @pl.when(jnp.logical_not(first_step))
    def _():
        xwin_ref[0:BLK, :] = xwin_ref[rows:rows + BLK, :]

    for n, ref in enumerate(xbc_refs):
        xwin_ref[BLK:BLK + rows, n * SSD_COL_BLOCK:(n + 1) * SSD_COL_BLOCK] = ref[...]

    out_t = lax.broadcasted_iota(jnp.int32, (BLK, 2 * BLK), 0)
    in_t = lax.broadcasted_iota(jnp.int32, (BLK, 2 * BLK), 1)
    shifts = [(in_t == out_t + (BLK - (CONV_WIDTH - 1 - k))).astype(BF16) for k in range(CONV_WIDTH)]

    lane = lax.broadcasted_iota(jnp.int32, (1, LANES), 1)
    lo_half = lane < HALF
    r_idx = lax.broadcasted_iota(jnp.int32, (BLK, BLK), 0)
    c_idx = lax.broadcasted_iota(jnp.int32, (BLK, BLK), 1)
    causal = c_idx <= r_idx
    tri_t = (r_idx <= c_idx).astype(BF16)
    expand = (lax.broadcasted_iota(jnp.int32, (LANES, SSM_WIDTH), 0)
              == lax.broadcasted_iota(jnp.int32, (LANES, SSM_WIDTH), 1) // HEAD_DIM).astype(BF16)
    a_neg_t = -jnp.exp(alog_ref[...])
    head_pad = jnp.zeros((LANES - SSM_HEADS, BLK), F32)

    def conv(t):
        for c0 in range(0, CONV_CH, CONV_SLAB):
            cs = slice(c0, c0 + CONV_SLAB)
            win = xwin_ref[t * BLK:(t + 2) * BLK, cs]
            acc = cb_ref[:, cs]
            for k in range(CONV_WIDTH):
                acc = acc + cw_ref[k:k + 1, cs] * _dot(shifts[k], win)
            u_ref[t, :, cs] = acc * _sigmoid(acc)

    def scan(t):
        rws = slice(t * BLK, (t + 1) * BLK)
        u = u_ref.at[t]
        v_t = dt_ref[rws, :].T[0:SSM_HEADS] + dtb_ref[...]
        dt_t = jnp.maximum(v_t, 0.0) + jnp.log1p(jnp.exp(-jnp.abs(v_t)))
        cum_t = sum(_dot(part, tri_t) for part in _split_bf16(dt_t * a_neg_t, 3))
        cum = jnp.concatenate([cum_t, head_pad], axis=0).T
        w_t = jnp.exp(cum_t[:, BLK - 1:BLK] - cum_t) * dt_t
        p_hi, p_lo = _split_bf16(jnp.exp(cum), 2)
        p_exp = _dot(p_hi, expand) + _dot(p_lo, expand)

        for g in range(SSM_GROUPS):
            b_g = u[:, SSM_WIDTH + g * D_STATE:SSM_WIDTH + (g + 1) * D_STATE]
            c_g = u[:, SSM_WIDTH + (SSM_GROUPS + g) * D_STATE:SSM_WIDTH + (SSM_GROUPS + g + 1) * D_STATE]
            c_bf = c_g.astype(BF16)
            cb = _dot_nt(c_bf, b_g.astype(BF16))
            b_t = b_g.T
            gated = []
            for pair in range(2 * g, 2 * g + 2):
                cols = slice(pair * LANES, (pair + 1) * LANES)
                x_f = u[:, cols]
                x_bf = x_f.astype(BF16)
                y_k, s_k = [], []
                for k in (2 * pair, 2 * pair + 1):
                    seg = cum[:, k:k + 1] - cum_t[k:k + 1, :]
                    decay = jnp.exp(jnp.where(causal, seg, -jnp.inf))
                    y_k.append(_dot((cb * decay * dt_t[k:k + 1, :]).astype(BF16), x_bf))
                    s_k.append(_dot((b_t * w_t[k:k + 1, :]).astype(BF16), x_bf))
                h_prev = h_ref[pair]
                y = (jnp.where(lo_half, y_k[0], y_k[1])
                     + _dot(c_bf, h_prev.astype(BF16)) * p_exp[:, cols]
                     + dskip_ref[:, cols] * x_f)
                h_ref[pair] = h_prev * p_exp[BLK - 1:BLK, cols] + jnp.where(lo_half, s_k[0], s_k[1])
                zb, zc = divmod(pair * LANES, SSD_COL_BLOCK)
                zf = z_refs[zb][rws, zc:zc + LANES].astype(F32)
                gated.append(y * (zf * _sigmoid(zf)))
            gg = jnp.concatenate(gated, axis=1)
            ms = jnp.mean(gg * gg, axis=-1, keepdims=True)
            gcols = slice(2 * g * LANES, (2 * g + 2) * LANES)
            o_ref[rws, gcols] = (gg * lax.rsqrt(ms + EPS) * nw_ref[:, gcols]).astype(o_ref.dtype)

    conv(0)
    for t in range(nchunk):
        if t + 1 < nchunk:
            conv(t + 1)
        scan(t)


def _ssd(proj, dt_raw, conv_w, conv_b, dt_bias, a_log, d_skip, norm_w, *, batch, seq):
    rows = min(MIX_ROWS, seq)
    steps = seq // rows
    per_head = lambda v: jnp.broadcast_to(v.astype(F32)[:, None], (SSM_HEADS, BLK))
    row = lambda b, c: b * steps + c
    col_blocks = range(Z_OFF // SSD_COL_BLOCK, PROJ_WIDTH // SSD_COL_BLOCK)
    proj_specs = [pl.BlockSpec((rows, SSD_COL_BLOCK), lambda b, c, n=n: (row(b, c), n)) for n in col_blocks]
    return pl.pallas_call(
        functools.partial(_ssd_kernel, rows=rows),
        grid=(batch, steps),
        in_specs=proj_specs + [
                  pl.BlockSpec((rows, LANES), lambda b, c: (row(b, c), 0)),
                  pl.BlockSpec((CONV_WIDTH, CONV_CH), lambda b, c: (0, 0)),
                  pl.BlockSpec((1, CONV_CH), lambda b, c: (0, 0)),
                  pl.BlockSpec((SSM_HEADS, BLK), lambda b, c: (0, 0)),
                  pl.BlockSpec((SSM_HEADS, BLK), lambda b, c: (0, 0)),
                  pl.BlockSpec((1, SSM_WIDTH), lambda b, c: (0, 0)),
                  pl.BlockSpec((1, SSM_WIDTH), lambda b, c: (0, 0))],
        out_specs=pl.BlockSpec((rows, SSM_WIDTH), lambda b, c: (row(b, c), 0)),
        out_shape=jax.ShapeDtypeStruct((batch * seq, SSM_WIDTH), BF16),
        scratch_shapes=[pltpu.VMEM((rows + BLK, CONV_CH), BF16),
                        pltpu.VMEM((rows // BLK, BLK, CONV_CH), F32),
                        pltpu.VMEM((SSM_HEADS // 2, D_STATE, LANES), F32)],
        compiler_params=_params(2),
        name="ssd_mixer",
    )(*([proj] * len(proj_specs)), dt_raw, conv_w.astype(F32), conv_b.astype(F32)[None], per_head(dt_bias),
      per_head(a_log), jnp.repeat(d_skip.astype(F32), HEAD_DIM)[None], norm_w.astype(F32)[None])


def _outproj_kernel(x_ref, a_ref, s_ref, wa_ref, ws_ref, nw_ref, o_ref, hw_ref, r_ref):
    d = o_ref.shape[1]
    ss = jnp.zeros((o_ref.shape[0], 1), F32)
    for c0 in range(0, d, OUT_SLAB):
        cols = slice(c0, c0 + OUT_SLAB)
        x1 = x_ref[:, cols] + _dot(a_ref[...], wa_ref[:, cols]) + _dot(s_ref[...], ws_ref[:, cols])
        o_ref[:, cols] = x1
        hw_ref[:, cols] = (x1 * nw_ref[:, cols]).astype(BF16)
        ss = ss + jnp.sum(x1 * x1, axis=-1, keepdims=True)
    r_ref[...] = jnp.broadcast_to(lax.rsqrt(ss * (1.0 / d) + EPS), r_ref.shape)


def _out_proj(x, attn, ssm, w_attn, w_ssm, ffn_norm_w):
    m, d = x.shape
    tm = min(ROW_TILE, m)
    rows = lambda width: pl.BlockSpec((tm, width), lambda i: (i, 0))
    return pl.pallas_call(
        _outproj_kernel,
        grid=(m // tm,),
        in_specs=[rows(d), rows(ATTN_WIDTH), rows(SSM_WIDTH),
                  _resident((ATTN_WIDTH, d)), _resident((SSM_WIDTH, d)), _resident((1, d))],
        out_specs=[rows(d), rows(d), rows(LANES)],
        out_shape=[jax.ShapeDtypeStruct((m, d), F32),
                   jax.ShapeDtypeStruct((m, d), BF16),
                   jax.ShapeDtypeStruct((m, LANES), F32)],
        compiler_params=_params(1),
        name="out_proj",
    )(x, attn, ssm, w_attn, w_ssm, ffn_norm_w)


def _ffn_kernel(x_ref, hw_ref, r_ref, wg_ref, wu_ref, wd_ref, o_ref):
    def tile(first):
        hw = hw_ref[...]
        r = r_ref[...]
        gate = _dot(hw, wg_ref[...])
        up = _dot(hw, wu_ref[...])
        act = jnp.concatenate(
            [((gate[:, c:c + LANES] * up[:, c:c + LANES]) * (r * r)
              * _sigmoid(gate[:, c:c + LANES] * r)).astype(BF16)
             for c in range(0, gate.shape[1], LANES)], axis=1)
        down = _dot(act, wd_ref[...])
        if first:
            o_ref[...] = x_ref[...] + down
        else:
            o_ref[...] += down

    first_tile = pl.program_id(1) == 0
    pl.when(first_tile)(functools.partial(tile, True))
    pl.when(jnp.logical_not(first_tile))(functools.partial(tile, False))


def _ffn(x, hw, r, w_gate, w_up, w_down):
    m, d = x.shape
    f = w_gate.shape[1]
    tm, tf = min(FFN_ROW_TILE, m), FF_TILE
    rows = lambda width: pl.BlockSpec((tm, width), lambda i, j: (i, 0))
    return pl.pallas_call(
        _ffn_kernel,
        grid=(m // tm, f // tf),
        in_specs=[rows(d), rows(d), rows(LANES),
                  pl.BlockSpec((d, tf), lambda i, j: (0, j)),
                  pl.BlockSpec((d, tf), lambda i, j: (0, j)),
                  pl.BlockSpec((tf, d), lambda i, j: (j, 0))],
        out_specs=rows(d),
        out_shape=jax.ShapeDtypeStruct((m, d), F32),
        compiler_params=_params(2),
        name="swiglu_ffn",
    )(x, hw, r, w_gate, w_up, w_down)


def _layer(x, p, *, batch, seq):
    (attn_norm_w, w_in, q_norm_w, k_norm_w, sinks, conv_w, conv_b, dt_bias, a_log, d_skip,
     ssm_norm_w, w_out, ffn_norm_w, w_gate, w_up, w_down) = p
    w_main = w_in[:, :PROJ_WIDTH].astype(BF16)
    w_dt = jnp.pad(w_in[:, PROJ_WIDTH:], ((0, 0), (0, LANES - SSM_HEADS))).astype(BF16)

    proj, dt_raw = _norm_inproj(x, attn_norm_w.astype(F32)[None], w_main, w_dt)
    attn = _attention(proj, q_norm_w, k_norm_w, sinks, seq=seq)
    ssm = _ssd(proj, dt_raw, conv_w, conv_b, dt_bias, a_log, d_skip, ssm_norm_w, batch=batch, seq=seq)
    x, hw, r = _out_proj(x, attn, ssm, w_out[:ATTN_WIDTH].astype(BF16), w_out[ATTN_WIDTH:].astype(BF16),
                         ffn_norm_w.astype(F32)[None])
    return _ffn(x, hw, r, w_gate.astype(BF16), w_up.astype(BF16), w_down.astype(BF16))


def kernel(x, attn_norm_w, w_in, q_norm_w, k_norm_w, sinks, conv_w, conv_b, dt_bias, a_log, d_skip,
           ssm_norm_w, w_out, ffn_norm_w, w_gate, w_up, w_down):
    batch, seq, d = x.shape
    stacked = (attn_norm_w, w_in, q_norm_w, k_norm_w, sinks, conv_w, conv_b, dt_bias, a_log, d_skip,
               ssm_norm_w, w_out, ffn_norm_w, w_gate, w_up, w_down)
    h = x.reshape(batch * seq, d)
    for layer in range(attn_norm_w.shape[0]):
        h = _layer(h, tuple(t[layer] for t in stacked), batch=batch, seq=seq)
    return h.reshape(batch, seq, d)
```

```python
import functools

import numpy as np
import jax
import jax.numpy as jnp
from jax import lax
from jax.experimental import pallas as pl
from jax.experimental.pallas import tpu as pltpu

F32 = jnp.float32
BF16 = jnp.bfloat16

D_MODEL = 2048
HEAD_DIM = 64
ATTN_WIDTH = 1024
N_Q_HEADS = ATTN_WIDTH // HEAD_DIM
N_KV_HEADS = 4
Q_PER_KV = N_Q_HEADS // N_KV_HEADS
KV_WIDTH = N_KV_HEADS * HEAD_DIM
BLK = 128
SSM_WIDTH = 1024
SSM_HEADS = SSM_WIDTH // HEAD_DIM
SSM_GROUPS = 4
D_STATE = 128
CONV_WIDTH = 4
CONV_CH = SSM_WIDTH + 2 * SSM_GROUPS * D_STATE
D_FF = 5632
EPS = 1e-6

LANES = 128
SUBLANES = 8
HALF = LANES // 2
MXU_DIM = 256
VMEM_LIMIT = 60 * 1024 * 1024

K_OFF = ATTN_WIDTH
V_OFF = K_OFF + KV_WIDTH
Z_OFF = V_OFF + KV_WIDTH
XBC_OFF = Z_OFF + SSM_WIDTH
PROJ_WIDTH = XBC_OFF + CONV_CH
SSD_COL_BLOCK = 512

ROW_TILE = 512
INPROJ_COL_TILE = 1536
OUT_SLAB = 512
FFN_ROW_TILE = 1024
FF_TILE = 512
MIX_ROWS = 512
CONV_SLAB = 512

ALIBI_SLOPES = tuple(float(s) for s in np.power(
    np.float32(2.0), -8.0 * np.arange(1, N_Q_HEADS + 1, dtype=np.float32) / N_Q_HEADS))


def _params(n_axes):
    return pltpu.CompilerParams(dimension_semantics=("arbitrary",) * n_axes,
                                vmem_limit_bytes=VMEM_LIMIT)


def _split_bf16(x, parts):
    out = []
    for _ in range(parts - 1):
        hi = x.astype(BF16)
        out.append(hi)
        x = x - hi.astype(F32)
    out.append(x.astype(BF16))
    return out


def _dot(a, b):
    return jnp.dot(a, b, preferred_element_type=F32)


def _dot_nt(a, b):
    return lax.dot_general(a, b, (((1,), (1,)), ((), ())), preferred_element_type=F32)


def _norm_inproj_kernel(x_ref, nw_ref, w_ref, wdt_ref, o_ref, odt_ref, h_ref):
    x = x_ref[...]
    ms = jnp.mean(x * x, axis=-1, keepdims=True)
    h_ref[...] = (x * lax.rsqrt(ms + EPS) * nw_ref[...]).astype(BF16)
    odt_ref[...] = _dot(h_ref[...], wdt_ref[...])
    for n in range(o_ref.shape[1] // INPROJ_COL_TILE):
        cols = slice(n * INPROJ_COL_TILE, (n + 1) * INPROJ_COL_TILE)
        o_ref[:, cols] = _dot(h_ref[...], w_ref[:, cols]).astype(o_ref.dtype)


def _resident(shape, layer=None):
    if layer is None:
        return pl.BlockSpec(shape, lambda i: (0,) * len(shape), pipeline_mode=pl.Buffered(1))
    return pl.BlockSpec((None,) + shape, lambda i: (layer,) + (0,) * len(shape), pipeline_mode=pl.Buffered(1))


def _norm_inproj(x, norm_w, w_main, w_dt, layer):
    m, d = x.shape
    n = w_main.shape[2]
    tm = min(ROW_TILE, m)
    return pl.pallas_call(
        _norm_inproj_kernel,
        grid=(m // tm,),
        in_specs=[pl.BlockSpec((tm, d), lambda i: (i, 0)),
                  _resident((1, d)),
                  _resident((d, n), layer),
                  _resident((d, LANES), layer)],
        out_specs=[pl.BlockSpec((tm, n), lambda i: (i, 0)),
                   pl.BlockSpec((tm, LANES), lambda i: (i, 0))],
        out_shape=[jax.ShapeDtypeStruct((m, n), BF16),
                   jax.ShapeDtypeStruct((m, LANES), F32)],
        scratch_shapes=[pltpu.VMEM((tm, d), BF16)],
        compiler_params=_params(1),
        name="norm_inproj",
    )(x, norm_w, w_main, w_dt)


def _head_sumsq(x):
    r = lax.broadcasted_iota(jnp.int32, (MXU_DIM, MXU_DIM), 0) // HEAD_DIM
    c = lax.broadcasted_iota(jnp.int32, (MXU_DIM, MXU_DIM), 1) // HEAD_DIM
    seg = (r == c).astype(BF16)
    outs = []
    for g in range(x.shape[1] // MXU_DIM):
        xg = x[:, g * MXU_DIM:(g + 1) * MXU_DIM]
        hi, lo = _split_bf16(xg * xg, 2)
        outs.append(_dot(hi, seg) + _dot(lo, seg))
    return outs[0] if len(outs) == 1 else jnp.concatenate(outs, axis=1)


def _attn_kernel(sink_ref, q_ref, kc_ref, vc_ref, kp_ref, vp_ref, qw_ref, kw_ref, o_ref,
                 qs_ref, kd_ref, vd_ref, *, rows, blocks_per_seq):
    step = pl.program_id(0)
    nblk = rows // BLK
    lo_half = lax.broadcasted_iota(jnp.int32, (1, LANES), 1) < HALF

    k_all = jnp.concatenate([kp_ref[...], kc_ref[...]], axis=0).astype(F32)
    v_all = jnp.concatenate([vp_ref[...], vc_ref[...]], axis=0).astype(F32)
    k_all = k_all * lax.rsqrt(_head_sumsq(k_all) * (1.0 / HEAD_DIM) + EPS) * kw_ref[...]
    for src, dst in ((k_all, kd_ref), (v_all, vd_ref)):
        for p in range(KV_WIDTH // LANES):
            a = src[:, p * LANES:(p + 1) * LANES]
            b = pltpu.roll(a, HALF, 1)
            dst[2 * p] = jnp.where(lo_half, a, b).astype(BF16)
            dst[2 * p + 1] = jnp.where(lo_half, b, a).astype(BF16)

    q = q_ref[...].astype(F32)
    q = q * lax.rsqrt(_head_sumsq(q) * (1.0 / HEAD_DIM) + EPS) * qw_ref[...]
    lo_q = (lax.broadcasted_iota(jnp.int32, (1, ATTN_WIDTH), 1) & HALF) == 0
    qs_ref[0] = jnp.where(lo_q, q, 0.0).astype(BF16)
    qs_ref[1] = jnp.where(lo_q, 0.0, q).astype(BF16)

    r_idx = lax.broadcasted_iota(jnp.int32, (BLK, 2 * BLK), 0)
    c_idx = lax.broadcasted_iota(jnp.int32, (BLK, 2 * BLK), 1)
    band = (c_idx > r_idx) & (c_idx <= r_idx + BLK)
    row_f = lax.broadcasted_iota(jnp.int32, (BLK, 1), 0).astype(F32)
    col_f = lax.broadcasted_iota(jnp.int32, (1, 2 * BLK), 1).astype(F32)

    def block(t, carry):
        r0 = pl.multiple_of(t * BLK, BLK)
        first = ((step * nblk + t) % blocks_per_seq) == 0
        valid = band & (c_idx >= jnp.where(first, BLK, 0))
        for j in range(N_KV_HEADS):
            c0 = j * Q_PER_KV * HEAD_DIM
            lhs = jnp.concatenate(
                [qs_ref[half, pl.ds(r0, BLK), c0 + grp * LANES:c0 + (grp + 1) * LANES]
                 for grp in range(2) for half in range(2)], axis=0)
            kj = kd_ref[j, pl.ds(r0, 2 * BLK), :]
            vj = vd_ref[j, pl.ds(r0, 2 * BLK), :]
            s = _dot_nt(lhs, kj)
            probs, inv_den = [], []
            for hh in range(Q_PER_KV):
                h = j * Q_PER_KV + hh
                slope = ALIBI_SLOPES[h]
                sh = s[hh * BLK:(hh + 1) * BLK] + slope * col_f
                sh = jnp.where(valid, sh, -jnp.inf)
                sink = sink_ref[h] + slope * (row_f + float(BLK))
                mx = jnp.maximum(jnp.max(sh, axis=-1, keepdims=True), sink)
                p = jnp.exp(sh - mx)
                den = jnp.sum(p, axis=-1, keepdims=True) + jnp.exp(sink - mx)
                probs.append(p.astype(BF16))
                inv_den.append(1.0 / den)
            o = _dot(jnp.concatenate(probs, axis=0), vj)
            for grp in range(2):
                oa = o[(2 * grp) * BLK:(2 * grp + 1) * BLK] * inv_den[2 * grp]
                ob = o[(2 * grp + 1) * BLK:(2 * grp + 2) * BLK] * inv_den[2 * grp + 1]
                o_ref[pl.ds(r0, BLK), c0 + grp * LANES:c0 + (grp + 1) * LANES] = (
                    jnp.where(lo_half, oa, ob).astype(o_ref.dtype))
        return carry

    lax.fori_loop(0, nblk, block, 0, unroll=2)


def _attention(proj, q_norm_w, k_norm_w, sinks, *, seq):
    m = proj.shape[0]
    rows = min(MIX_ROWS, seq)
    nblk = rows // BLK
    qw = (jnp.tile(q_norm_w.astype(F32), N_Q_HEADS) * (HEAD_DIM ** -0.5))[None]
    kw = jnp.tile(k_norm_w.astype(F32), N_KV_HEADS)[None]
    kb, vb = K_OFF // KV_WIDTH, V_OFF // KV_WIDTH
    prev = lambda i: jnp.maximum(i * nblk - 1, 0)
    return pl.pallas_call(
        functools.partial(_attn_kernel, rows=rows, blocks_per_seq=seq // BLK),
        grid=(m // rows,),
        in_specs=[pl.BlockSpec(memory_space=pltpu.SMEM),
                  pl.BlockSpec((rows, ATTN_WIDTH), lambda i: (i, 0)),
                  pl.BlockSpec((rows, KV_WIDTH), lambda i: (i, kb)),
                  pl.BlockSpec((rows, KV_WIDTH), lambda i: (i, vb)),
                  pl.BlockSpec((BLK, KV_WIDTH), lambda i: (prev(i), kb)),
                  pl.BlockSpec((BLK, KV_WIDTH), lambda i: (prev(i), vb)),
                  pl.BlockSpec((1, ATTN_WIDTH), lambda i: (0, 0)),
                  pl.BlockSpec((1, KV_WIDTH), lambda i: (0, 0))],
        out_specs=pl.BlockSpec((rows, ATTN_WIDTH), lambda i: (i, 0)),
        out_shape=jax.ShapeDtypeStruct((m, ATTN_WIDTH), BF16),
        scratch_shapes=[pltpu.VMEM((2, rows, ATTN_WIDTH), BF16),
                        pltpu.VMEM((N_KV_HEADS, rows + BLK, LANES), BF16),
                        pltpu.VMEM((N_KV_HEADS, rows + BLK, LANES), BF16)],
        compiler_params=_params(1),
        name="swa_attention",
    )(sinks.astype(F32), proj, proj, proj, proj, proj, qw, kw)


def _sigmoid(x):
    return 1.0 / (1.0 + jnp.exp(-x))


def _ssd_kernel(*refs, rows):
    nz, nx = SSM_WIDTH // SSD_COL_BLOCK, CONV_CH // SSD_COL_BLOCK
    z_refs, xbc_refs = refs[:nz], refs[nz:nz + nx]
    (dt_ref, cw_ref, cb_ref, dtb_ref, alog_ref, dskip_ref, nw_ref, o_ref,
     xwin_ref, u_ref, h_ref) = refs[nz + nx:]
    nchunk = rows // BLK
    first_step = pl.program_id(1) == 0

    @pl.when(first_step)
    def _():
        xwin_ref[0:BLK, :] = jnp.zeros((BLK, CONV_CH), BF16)
        h_ref[...] = jnp.zeros_like(h_ref)

    @pl.when(jnp.logical_not(first_step))
    def _():
        xwin_ref[0:BLK, :] = xwin_ref[rows:rows + BLK, :]

    for n, ref in enumerate(xbc_refs):
        xwin_ref[BLK:BLK + rows, n * SSD_COL_BLOCK:(n + 1) * SSD_COL_BLOCK] = ref[...]

    out_t = lax.broadcasted_iota(jnp.int32, (BLK, 2 * BLK), 0)
    in_t = lax.broadcasted_iota(jnp.int32, (BLK, 2 * BLK), 1)
    shifts = [(in_t == out_t + (BLK - (CONV_WIDTH - 1 - k))).astype(BF16) for k in range(CONV_WIDTH)]

    lane = lax.broadcasted_iota(jnp.int32, (1, LANES), 1)
    lo_half = lane < HALF
    r_idx = lax.broadcasted_iota(jnp.int32, (BLK, BLK), 0)
    c_idx = lax.broadcasted_iota(jnp.int32, (BLK, BLK), 1)
    causal = c_idx <= r_idx
    tri = causal.astype(BF16)
    expand = (lax.broadcasted_iota(jnp.int32, (LANES, SSM_WIDTH), 0)
              == lax.broadcasted_iota(jnp.int32, (LANES, SSM_WIDTH), 1) // HEAD_DIM).astype(BF16)
    a_neg = jnp.where(lane < SSM_HEADS, -jnp.exp(alog_ref[...]), 0.0)

    def chunk(t, carry):
        r0 = pl.multiple_of(t * BLK, BLK)
        rws = pl.ds(r0, BLK)

        for c0 in range(0, CONV_CH, CONV_SLAB):
            cs = slice(c0, c0 + CONV_SLAB)
            win = xwin_ref[pl.ds(r0, 2 * BLK), cs]
            acc = cb_ref[:, cs]
            for k in range(CONV_WIDTH):
                acc = acc + cw_ref[k:k + 1, cs] * _dot(shifts[k], win)
            u_ref[:, cs] = acc * _sigmoid(acc)

        v = dt_ref[rws, :] + dtb_ref[...]
        dt = jnp.maximum(v, 0.0) + jnp.log1p(jnp.exp(-jnp.abs(v)))
        da = dt * a_neg
        cum = sum(_dot(tri, part) for part in _split_bf16(da, 3))
        cum_t = cum.T[0:SSM_HEADS]
        dt_t = dt.T[0:SSM_HEADS]
        w_t = jnp.exp(cum_t[:, BLK - 1:BLK] - cum_t) * dt_t
        p_hi, p_lo = _split_bf16(jnp.exp(cum), 2)
        p_exp = _dot(p_hi, expand) + _dot(p_lo, expand)

        for g in range(SSM_GROUPS):
            b_g = u_ref[:, SSM_WIDTH + g * D_STATE:SSM_WIDTH + (g + 1) * D_STATE]
            c_g = u_ref[:, SSM_WIDTH + (SSM_GROUPS + g) * D_STATE:SSM_WIDTH + (SSM_GROUPS + g + 1) * D_STATE]
            c_bf = c_g.astype(BF16)
            cb = _dot_nt(c_bf, b_g.astype(BF16))
            b_t = b_g.T
            gated = []
            for pair in range(2 * g, 2 * g + 2):
                cols = slice(pair * LANES, (pair + 1) * LANES)
                x_f = u_ref[:, cols]
                x_bf = x_f.astype(BF16)
                y_k, s_k = [], []
                for k in (2 * pair, 2 * pair + 1):
                    seg = cum[:, k:k + 1] - cum_t[k:k + 1, :]
                    decay = jnp.exp(jnp.where(causal, seg, -jnp.inf))
                    y_k.append(_dot((cb * decay * dt_t[k:k + 1, :]).astype(BF16), x_bf))
                    s_k.append(_dot((b_t * w_t[k:k + 1, :]).astype(BF16), x_bf))
                h_prev = h_ref[pair]
                y = (jnp.where(lo_half, y_k[0], y_k[1])
                     + _dot(c_bf, h_prev.astype(BF16)) * p_exp[:, cols]
                     + dskip_ref[:, cols] * x_f)
                h_ref[pair] = h_prev * p_exp[BLK - 1:BLK, cols] + jnp.where(lo_half, s_k[0], s_k[1])
                zb, zc = divmod(pair * LANES, SSD_COL_BLOCK)
                zf = z_refs[zb][rws, zc:zc + LANES].astype(F32)
                gated.append(y * (zf * _sigmoid(zf)))
            gg = jnp.concatenate(gated, axis=1)
            ms = jnp.mean(gg * gg, axis=-1, keepdims=True)
            gcols = slice(2 * g * LANES, (2 * g + 2) * LANES)
            o_ref[rws, gcols] = (gg * lax.rsqrt(ms + EPS) * nw_ref[:, gcols]).astype(o_ref.dtype)
        return carry

    lax.fori_loop(0, nchunk, chunk, 0, unroll=2)


def _ssd(proj, dt_raw, conv_w, conv_b, dt_bias, a_log, d_skip, norm_w, *, batch, seq):
    rows = min(MIX_ROWS, seq)
    steps = seq // rows
    pad_heads = lambda v: jnp.pad(v.astype(F32), (0, LANES - SSM_HEADS))[None]
    row = lambda b, c: b * steps + c
    col_blocks = range(Z_OFF // SSD_COL_BLOCK, PROJ_WIDTH // SSD_COL_BLOCK)
    proj_specs = [pl.BlockSpec((rows, SSD_COL_BLOCK), lambda b, c, n=n: (row(b, c), n)) for n in col_blocks]
    return pl.pallas_call(
        functools.partial(_ssd_kernel, rows=rows),
        grid=(batch, steps),
        in_specs=proj_specs + [
                  pl.BlockSpec((rows, LANES), lambda b, c: (row(b, c), 0)),
                  pl.BlockSpec((CONV_WIDTH, CONV_CH), lambda b, c: (0, 0)),
                  pl.BlockSpec((1, CONV_CH), lambda b, c: (0, 0)),
                  pl.BlockSpec((1, LANES), lambda b, c: (0, 0)),
                  pl.BlockSpec((1, LANES), lambda b, c: (0, 0)),
                  pl.BlockSpec((1, SSM_WIDTH), lambda b, c: (0, 0)),
                  pl.BlockSpec((1, SSM_WIDTH), lambda b, c: (0, 0))],
        out_specs=pl.BlockSpec((rows, SSM_WIDTH), lambda b, c: (row(b, c), 0)),
        out_shape=jax.ShapeDtypeStruct((batch * seq, SSM_WIDTH), BF16),
        scratch_shapes=[pltpu.VMEM((rows + BLK, CONV_CH), BF16),
                        pltpu.VMEM((BLK, CONV_CH), F32),
                        pltpu.VMEM((SSM_HEADS // 2, D_STATE, LANES), F32)],
        compiler_params=_params(2),
        name="ssd_mixer",
    )(*([proj] * len(proj_specs)), dt_raw, conv_w.astype(F32), conv_b.astype(F32)[None], pad_heads(dt_bias),
      pad_heads(a_log), jnp.repeat(d_skip.astype(F32), HEAD_DIM)[None], norm_w.astype(F32)[None])


def _outproj_kernel(x_ref, a_ref, s_ref, wa_ref, ws_ref, nw_ref, o_ref, hw_ref, r_ref):
    d = o_ref.shape[1]
    ss = jnp.zeros((o_ref.shape[0], 1), F32)
    for c0 in range(0, d, OUT_SLAB):
        cols = slice(c0, c0 + OUT_SLAB)
        x1 = x_ref[:, cols] + _dot(a_ref[...], wa_ref[:, cols]) + _dot(s_ref[...], ws_ref[:, cols])
        o_ref[:, cols] = x1
        hw_ref[:, cols] = (x1 * nw_ref[:, cols]).astype(BF16)
        ss = ss + jnp.sum(x1 * x1, axis=-1, keepdims=True)
    r_ref[...] = jnp.broadcast_to(lax.rsqrt(ss * (1.0 / d) + EPS), r_ref.shape)


def _out_proj(x, attn, ssm, w_out, ffn_norm_w, layer):
    m, d = x.shape
    tm = min(ROW_TILE, m)
    rows = lambda width: pl.BlockSpec((tm, width), lambda i: (i, 0))
    w_half = lambda half: pl.BlockSpec((None, w_out.shape[1] // 2, d), lambda i: (layer, half, 0),
                                       pipeline_mode=pl.Buffered(1))
    return pl.pallas_call(
        _outproj_kernel,
        grid=(m // tm,),
        in_specs=[rows(d), rows(ATTN_WIDTH), rows(SSM_WIDTH),
                  w_half(0), w_half(1), _resident((1, d))],
        out_specs=[rows(d), rows(d), rows(LANES)],
        out_shape=[jax.ShapeDtypeStruct((m, d), F32),
                   jax.ShapeDtypeStruct((m, d), BF16),
                   jax.ShapeDtypeStruct((m, LANES), F32)],
        compiler_params=_params(1),
        name="out_proj",
    )(x, attn, ssm, w_out, w_out, ffn_norm_w)


def _ffn_kernel(x_ref, hw_ref, r_ref, wg_ref, wu_ref, wd_ref, o_ref):
    def tile(first):
        hw = hw_ref[...]
        r = r_ref[...]
        gate = _dot(hw, wg_ref[...])
        up = _dot(hw, wu_ref[...])
        act = jnp.concatenate(
            [((gate[:, c:c + LANES] * up[:, c:c + LANES]) * (r * r)
              * _sigmoid(gate[:, c:c + LANES] * r)).astype(BF16)
             for c in range(0, gate.shape[1], LANES)], axis=1)
        down = _dot(act, wd_ref[...])
        if first:
            o_ref[...] = x_ref[...] + down
        else:
            o_ref[...] += down

    first_tile = pl.program_id(1) == 0
    pl.when(first_tile)(functools.partial(tile, True))
    pl.when(jnp.logical_not(first_tile))(functools.partial(tile, False))


def _ffn(x, hw, r, w_gate, w_up, w_down, layer):
    m, d = x.shape
    f = w_gate.shape[2]
    tm, tf = min(FFN_ROW_TILE, m), FF_TILE
    rows = lambda width: pl.BlockSpec((tm, width), lambda i, j: (i, 0))
    return pl.pallas_call(
        _ffn_kernel,
        grid=(m // tm, f // tf),
        in_specs=[rows(d), rows(d), rows(LANES),
                  pl.BlockSpec((None, d, tf), lambda i, j: (layer, 0, j)),
                  pl.BlockSpec((None, d, tf), lambda i, j: (layer, 0, j)),
                  pl.BlockSpec((None, tf, d), lambda i, j: (layer, j, 0))],
        out_specs=rows(d),
        out_shape=jax.ShapeDtypeStruct((m, d), F32),
        compiler_params=_params(2),
        name="swiglu_ffn",
    )(x, hw, r, w_gate, w_up, w_down)


def _layer(x, small, big, layer, *, batch, seq):
    (attn_norm_w, q_norm_w, k_norm_w, sinks, conv_w, conv_b, dt_bias, a_log, d_skip, ssm_norm_w, ffn_norm_w) = small
    w_main, w_dt, w_out, w_gate, w_up, w_down = big
    proj, dt_raw = _norm_inproj(x, attn_norm_w.astype(F32)[None], w_main, w_dt, layer)
    attn = _attention(proj, q_norm_w, k_norm_w, sinks, seq=seq)
    ssm = _ssd(proj, dt_raw, conv_w, conv_b, dt_bias, a_log, d_skip, ssm_norm_w, batch=batch, seq=seq)
    x, hw, r = _out_proj(x, attn, ssm, w_out, ffn_norm_w.astype(F32)[None], layer)
    return _ffn(x, hw, r, w_gate, w_up, w_down, layer)


def kernel(x, attn_norm_w, w_in, q_norm_w, k_norm_w, sinks, conv_w, conv_b, dt_bias, a_log, d_skip,
           ssm_norm_w, w_out, ffn_norm_w, w_gate, w_up, w_down):
    batch, seq, d = x.shape
    big = (w_in[:, :, :PROJ_WIDTH].astype(BF16),
           jnp.pad(w_in[:, :, PROJ_WIDTH:], ((0, 0), (0, 0), (0, LANES - SSM_HEADS))).astype(BF16),
           w_out.astype(BF16), w_gate.astype(BF16), w_up.astype(BF16), w_down.astype(BF16))
    stacked_small = (attn_norm_w, q_norm_w, k_norm_w, sinks, conv_w, conv_b, dt_bias, a_log, d_skip,
                     ssm_norm_w, ffn_norm_w)
    h = x.reshape(batch * seq, d)
    for layer in range(attn_norm_w.shape[0]):
        h = _layer(h, tuple(t[layer] for t in stacked_small), big, layer, batch=batch, seq=seq)
    return h.reshape(batch, seq, d)
```

```python
import functools

import numpy as np
import jax
import jax.numpy as jnp
from jax import lax
from jax.experimental import pallas as pl
from jax.experimental.pallas import tpu as pltpu

F32 = jnp.float32
BF16 = jnp.bfloat16

D_MODEL = 2048
HEAD_DIM = 64
ATTN_WIDTH = 1024
N_Q_HEADS = ATTN_WIDTH // HEAD_DIM
N_KV_HEADS = 4
Q_PER_KV = N_Q_HEADS // N_KV_HEADS
KV_WIDTH = N_KV_HEADS * HEAD_DIM
BLK = 128
SSM_WIDTH = 1024
SSM_HEADS = SSM_WIDTH // HEAD_DIM
SSM_GROUPS = 4
D_STATE = 128
CONV_WIDTH = 4
CONV_CH = SSM_WIDTH + 2 * SSM_GROUPS * D_STATE
D_FF = 5632
EPS = 1e-6

LANES = 128
SUBLANES = 8
BF16_SUBLANES = 16
HALF = LANES // 2
MXU_DIM = 256
VMEM_LIMIT = 60 * 1024 * 1024

K_OFF = ATTN_WIDTH
V_OFF = K_OFF + KV_WIDTH
Z_OFF = V_OFF + KV_WIDTH
XBC_OFF = Z_OFF + SSM_WIDTH
PROJ_WIDTH = XBC_OFF + CONV_CH
SSD_COL_BLOCK = 512

ROW_TILE = 512
INPROJ_COL_TILE = 1536
OUT_SLAB = 512
FFN_ROW_TILE = 1024
FF_TILE = 512
MIX_ROWS = 512
CONV_SLAB = 512

ALIBI_SLOPES = tuple(float(s) for s in np.power(
    np.float32(2.0), -8.0 * np.arange(1, N_Q_HEADS + 1, dtype=np.float32) / N_Q_HEADS))


def _params(n_axes):
    return pltpu.CompilerParams(dimension_semantics=("arbitrary",) * n_axes,
                                vmem_limit_bytes=VMEM_LIMIT)


def _split_bf16(x, parts):
    out = []
    for _ in range(parts - 1):
        hi = x.astype(BF16)
        out.append(hi)
        x = x - hi.astype(F32)
    out.append(x.astype(BF16))
    return out


def _dot(a, b):
    return jnp.dot(a, b, preferred_element_type=F32)


def _dot_nt(a, b):
    return lax.dot_general(a, b, (((1,), (1,)), ((), ())), preferred_element_type=F32)


def _norm_inproj_kernel(x_ref, nw_ref, w_ref, wdt_ref, wg_ref, wu_ref, wd_ref,
                        o_ref, odt_ref, wg_out, wu_out, wd_out, h_ref, *, down_steps):
    wg_out[...] = wg_ref[...].astype(BF16)
    wu_out[...] = wu_ref[...].astype(BF16)

    @pl.when(pl.program_id(0) < down_steps)
    def _():
        wd_out[...] = wd_ref[...].astype(BF16)

    x = x_ref[...]
    ms = jnp.mean(x * x, axis=-1, keepdims=True)
    h_ref[...] = (x * lax.rsqrt(ms + EPS) * nw_ref[...]).astype(BF16)
    odt_ref[...] = _dot(h_ref[...], wdt_ref[...])
    for n in range(o_ref.shape[1] // INPROJ_COL_TILE):
        cols = slice(n * INPROJ_COL_TILE, (n + 1) * INPROJ_COL_TILE)
        o_ref[:, cols] = _dot(h_ref[...], w_ref[:, cols]).astype(o_ref.dtype)


def _resident(shape, layer=None):
    if layer is None:
        return pl.BlockSpec(shape, lambda i: (0,) * len(shape), pipeline_mode=pl.Buffered(1))
    return pl.BlockSpec((None,) + shape, lambda i: (layer,) + (0,) * len(shape), pipeline_mode=pl.Buffered(1))


def _norm_inproj(x, norm_w, w_in, w_dt, w_gate, w_up, w_down, layer):
    m, d = x.shape
    n, f = PROJ_WIDTH, w_gate.shape[2]
    tm = min(ROW_TILE, m)
    steps = m // tm
    down_steps = steps
    while f % down_steps or (f // down_steps) % BF16_SUBLANES:
        down_steps //= 2
    up_rows, down_rows = d // steps, f // down_steps
    assert d % steps == 0 and up_rows % BF16_SUBLANES == 0
    down_blk = lambda i: jnp.minimum(i, down_steps - 1)
    return pl.pallas_call(
        functools.partial(_norm_inproj_kernel, down_steps=down_steps),
        grid=(steps,),
        in_specs=[pl.BlockSpec((tm, d), lambda i: (i, 0)),
                  _resident((1, d)),
                  _resident((d, n), layer),
                  _resident((d, LANES), layer),
                  pl.BlockSpec((None, up_rows, f), lambda i: (layer, i, 0)),
                  pl.BlockSpec((None, up_rows, f), lambda i: (layer, i, 0)),
                  pl.BlockSpec((None, down_rows, d), lambda i: (layer, down_blk(i), 0))],
        out_specs=[pl.BlockSpec((tm, n), lambda i: (i, 0)),
                   pl.BlockSpec((tm, LANES), lambda i: (i, 0)),
                   pl.BlockSpec((up_rows, f), lambda i: (i, 0)),
                   pl.BlockSpec((up_rows, f), lambda i: (i, 0)),
                   pl.BlockSpec((down_rows, d), lambda i: (down_blk(i), 0))],
        out_shape=[jax.ShapeDtypeStruct((m, n), BF16),
                   jax.ShapeDtypeStruct((m, LANES), F32),
                   jax.ShapeDtypeStruct((d, f), BF16),
                   jax.ShapeDtypeStruct((d, f), BF16),
                   jax.ShapeDtypeStruct((f, d), BF16)],
        scratch_shapes=[pltpu.VMEM((tm, d), BF16)],
        compiler_params=_params(1),
        name="norm_inproj",
    )(x, norm_w, w_in, w_dt, w_gate, w_up, w_down)


def _head_sumsq(x):
    r = lax.broadcasted_iota(jnp.int32, (MXU_DIM, MXU_DIM), 0) // HEAD_DIM
    c = lax.broadcasted_iota(jnp.int32, (MXU_DIM, MXU_DIM), 1) // HEAD_DIM
    seg = (r == c).astype(BF16)
    outs = []
    for g in range(x.shape[1] // MXU_DIM):
        xg = x[:, g * MXU_DIM:(g + 1) * MXU_DIM]
        hi, lo = _split_bf16(xg * xg, 2)
        outs.append(_dot(hi, seg) + _dot(lo, seg))
    return outs[0] if len(outs) == 1 else jnp.concatenate(outs, axis=1)


def _attn_kernel(sink_ref, q_ref, kc_ref, vc_ref, kp_ref, vp_ref, qw_ref, kw_ref, o_ref,
                 qs_ref, kd_ref, vd_ref, *, rows, blocks_per_seq):
    step = pl.program_id(0)
    nblk = rows // BLK
    lo_half = lax.broadcasted_iota(jnp.int32, (1, LANES), 1) < HALF

    k_all = jnp.concatenate([kp_ref[...], kc_ref[...]], axis=0).astype(F32)
    v_all = jnp.concatenate([vp_ref[...], vc_ref[...]], axis=0).astype(F32)
    k_all = k_all * lax.rsqrt(_head_sumsq(k_all) * (1.0 / HEAD_DIM) + EPS) * kw_ref[...]
    for src, dst in ((k_all, kd_ref), (v_all, vd_ref)):
        for p in range(KV_WIDTH // LANES):
            a = src[:, p * LANES:(p + 1) * LANES]
            b = pltpu.roll(a, HALF, 1)
            dst[2 * p] = jnp.where(lo_half, a, b).astype(BF16)
            dst[2 * p + 1] = jnp.where(lo_half, b, a).astype(BF16)

    q = q_ref[...].astype(F32)
    q = q * lax.rsqrt(_head_sumsq(q) * (1.0 / HEAD_DIM) + EPS) * qw_ref[...]
    lo_q = (lax.broadcasted_iota(jnp.int32, (1, ATTN_WIDTH), 1) & HALF) == 0
    qs_ref[0] = jnp.where(lo_q, q, 0.0).astype(BF16)
    qs_ref[1] = jnp.where(lo_q, 0.0, q).astype(BF16)

    r_idx = lax.broadcasted_iota(jnp.int32, (BLK, 2 * BLK), 0)
    c_idx = lax.broadcasted_iota(jnp.int32, (BLK, 2 * BLK), 1)
    band = (c_idx > r_idx) & (c_idx <= r_idx + BLK)
    row_f = lax.broadcasted_iota(jnp.int32, (BLK, 1), 0).astype(F32)
    col_f = lax.broadcasted_iota(jnp.int32, (1, 2 * BLK), 1).astype(F32)

    def block(t, carry):
        r0 = pl.multiple_of(t * BLK, BLK)
        first = ((step * nblk + t) % blocks_per_seq) == 0
        valid = band & (c_idx >= jnp.where(first, BLK, 0))
        for j in range(N_KV_HEADS):
            c0 = j * Q_PER_KV * HEAD_DIM
            lhs = jnp.concatenate(
                [qs_ref[half, pl.ds(r0, BLK), c0 + grp * LANES:c0 + (grp + 1) * LANES]
                 for grp in range(2) for half in range(2)], axis=0)
            kj = kd_ref[j, pl.ds(r0, 2 * BLK), :]
            vj = vd_ref[j, pl.ds(r0, 2 * BLK), :]
            s = _dot_nt(lhs, kj)
            probs, inv_den = [], []
            for hh in range(Q_PER_KV):
                h = j * Q_PER_KV + hh
                slope = ALIBI_SLOPES[h]
                sh = s[hh * BLK:(hh + 1) * BLK] + slope * col_f
                sh = jnp.where(valid, sh, -jnp.inf)
                sink = sink_ref[h] + slope * (row_f + float(BLK))
                mx = jnp.maximum(jnp.max(sh, axis=-1, keepdims=True), sink)
                p = jnp.exp(sh - mx)
                den = jnp.sum(p, axis=-1, keepdims=True) + jnp.exp(sink - mx)
                probs.append(p.astype(BF16))
                inv_den.append(1.0 / den)
            o = _dot(jnp.concatenate(probs, axis=0), vj)
            for grp in range(2):
                oa = o[(2 * grp) * BLK:(2 * grp + 1) * BLK] * inv_den[2 * grp]
                ob = o[(2 * grp + 1) * BLK:(2 * grp + 2) * BLK] * inv_den[2 * grp + 1]
                o_ref[pl.ds(r0, BLK), c0 + grp * LANES:c0 + (grp + 1) * LANES] = (
                    jnp.where(lo_half, oa, ob).astype(o_ref.dtype))
        return carry

    lax.fori_loop(0, nblk, block, 0, unroll=2)


def _attention(proj, q_norm_w, k_norm_w, sinks, *, seq):
    m = proj.shape[0]
    rows = min(MIX_ROWS, seq)
    nblk = rows // BLK
    qw = (jnp.tile(q_norm_w.astype(F32), N_Q_HEADS) * (HEAD_DIM ** -0.5))[None]
    kw = jnp.tile(k_norm_w.astype(F32), N_KV_HEADS)[None]
    kb, vb = K_OFF // KV_WIDTH, V_OFF // KV_WIDTH
    prev = lambda i: jnp.maximum(i * nblk - 1, 0)
    return pl.pallas_call(
        functools.partial(_attn_kernel, rows=rows, blocks_per_seq=seq // BLK),
        grid=(m // rows,),
        in_specs=[pl.BlockSpec(memory_space=pltpu.SMEM),
                  pl.BlockSpec((rows, ATTN_WIDTH), lambda i: (i, 0)),
                  pl.BlockSpec((rows, KV_WIDTH), lambda i: (i, kb)),
                  pl.BlockSpec((rows, KV_WIDTH), lambda i: (i, vb)),
                  pl.BlockSpec((BLK, KV_WIDTH), lambda i: (prev(i), kb)),
                  pl.BlockSpec((BLK, KV_WIDTH), lambda i: (prev(i), vb)),
                  pl.BlockSpec((1, ATTN_WIDTH), lambda i: (0, 0)),
                  pl.BlockSpec((1, KV_WIDTH), lambda i: (0, 0))],
        out_specs=pl.BlockSpec((rows, ATTN_WIDTH), lambda i: (i, 0)),
        out_shape=jax.ShapeDtypeStruct((m, ATTN_WIDTH), BF16),
        scratch_shapes=[pltpu.VMEM((2, rows, ATTN_WIDTH), BF16),
                        pltpu.VMEM((N_KV_HEADS, rows + BLK, LANES), BF16),
                        pltpu.VMEM((N_KV_HEADS, rows + BLK, LANES), BF16)],
        compiler_params=_params(1),
        name="swa_attention",
    )(sinks.astype(F32), proj, proj, proj, proj, proj, qw, kw)


def _sigmoid(x):
    return 1.0 / (1.0 + jnp.exp(-x))


def _ssd_kernel(*refs, rows):
    nz, nx = SSM_WIDTH // SSD_COL_BLOCK, CONV_CH // SSD_COL_BLOCK
    z_refs, xbc_refs = refs[:nz], refs[nz:nz + nx]
    (dt_ref, cw_ref, cb_ref, dtb_ref, alog_ref, dskip_ref, nw_ref, o_ref,
     xwin_ref, u_ref, h_ref) = refs[nz + nx:]
    nchunk = rows // BLK
    first_step = pl.program_id(1) == 0

    @pl.when(first_step)
    def _():
        xwin_ref[0:BLK, :] = jnp.zeros((BLK, CONV_CH), BF16)
        h_ref[...] = jnp.zeros_like(h_ref)

    @pl.when(jnp.logical_not(first_step))
    def _():
        xwin_ref[0:BLK, :] = xwin_ref[rows:rows + BLK, :]

    for n, ref in enumerate(xbc_refs):
        xwin_ref[BLK:BLK + rows, n * SSD_COL_BLOCK:(n + 1) * SSD_COL_BLOCK] = ref[...]

    out_t = lax.broadcasted_iota(jnp.int32, (BLK, 2 * BLK), 0)
    in_t = lax.broadcasted_iota(jnp.int32, (BLK, 2 * BLK), 1)
    shifts = [(in_t == out_t + (BLK - (CONV_WIDTH - 1 - k))).astype(BF16) for k in range(CONV_WIDTH)]

    lane = lax.broadcasted_iota(jnp.int32, (1, LANES), 1)
    lo_half = lane < HALF
    r_idx = lax.broadcasted_iota(jnp.int32, (BLK, BLK), 0)
    c_idx = lax.broadcasted_iota(jnp.int32, (BLK, BLK), 1)
    causal = c_idx <= r_idx
    tri = causal.astype(BF16)
    expand = (lax.broadcasted_iota(jnp.int32, (LANES, SSM_WIDTH), 0)
              == lax.broadcasted_iota(jnp.int32, (LANES, SSM_WIDTH), 1) // HEAD_DIM).astype(BF16)
    a_neg = jnp.where(lane < SSM_HEADS, -jnp.exp(alog_ref[...]), 0.0)

    def chunk(t, carry):
        r0 = pl.multiple_of(t * BLK, BLK)
        rws = pl.ds(r0, BLK)

        for c0 in range(0, CONV_CH, CONV_SLAB):
            cs = slice(c0, c0 + CONV_SLAB)
            win = xwin_ref[pl.ds(r0, 2 * BLK), cs]
            acc = cb_ref[:, cs]
            for k in range(CONV_WIDTH):
                acc = acc + cw_ref[k:k + 1, cs] * _dot(shifts[k], win)
            u_ref[:, cs] = acc * _sigmoid(acc)

        v = dt_ref[rws, :] + dtb_ref[...]
        dt = jnp.maximum(v, 0.0) + jnp.log1p(jnp.exp(-jnp.abs(v)))
        da = dt * a_neg
        cum = sum(_dot(tri, part) for part in _split_bf16(da, 3))
        cum_t = cum.T[0:SSM_HEADS]
        dt_t = dt.T[0:SSM_HEADS]
        w_t = jnp.exp(cum_t[:, BLK - 1:BLK] - cum_t) * dt_t
        p_hi, p_lo = _split_bf16(jnp.exp(cum), 2)
        p_exp = _dot(p_hi, expand) + _dot(p_lo, expand)

        for g in range(SSM_GROUPS):
            b_g = u_ref[:, SSM_WIDTH + g * D_STATE:SSM_WIDTH + (g + 1) * D_STATE]
            c_g = u_ref[:, SSM_WIDTH + (SSM_GROUPS + g) * D_STATE:SSM_WIDTH + (SSM_GROUPS + g + 1) * D_STATE]
            c_bf = c_g.astype(BF16)
            cb = _dot_nt(c_bf, b_g.astype(BF16))
            b_t = b_g.T
            gated = []
            for pair in range(2 * g, 2 * g + 2):
                cols = slice(pair * LANES, (pair + 1) * LANES)
                x_f = u_ref[:, cols]
                x_bf = x_f.astype(BF16)
                y_k, s_k = [], []
                for k in (2 * pair, 2 * pair + 1):
                    seg = cum[:, k:k + 1] - cum_t[k:k + 1, :]
                    decay = jnp.exp(jnp.where(causal, seg, -jnp.inf))
                    y_k.append(_dot((cb * decay * dt_t[k:k + 1, :]).astype(BF16), x_bf))
                    s_k.append(_dot((b_t * w_t[k:k + 1, :]).astype(BF16), x_bf))
                h_prev = h_ref[pair]
                y = (jnp.where(lo_half, y_k[0], y_k[1])
                     + _dot(c_bf, h_prev.astype(BF16)) * p_exp[:, cols]
                     + dskip_ref[:, cols] * x_f)
                h_ref[pair] = h_prev * p_exp[BLK - 1:BLK, cols] + jnp.where(lo_half, s_k[0], s_k[1])
                zb, zc = divmod(pair * LANES, SSD_COL_BLOCK)
                zf = z_refs[zb][rws, zc:zc + LANES].astype(F32)
                gated.append(y * (zf * _sigmoid(zf)))
            gg = jnp.concatenate(gated, axis=1)
            ms = jnp.mean(gg * gg, axis=-1, keepdims=True)
            gcols = slice(2 * g * LANES, (2 * g + 2) * LANES)
            o_ref[rws, gcols] = (gg * lax.rsqrt(ms + EPS) * nw_ref[:, gcols]).astype(o_ref.dtype)
        return carry

    lax.fori_loop(0, nchunk, chunk, 0, unroll=2)


def _ssd(proj, dt_raw, conv_w, conv_b, dt_bias, a_log, d_skip, norm_w, *, batch, seq):
    rows = min(MIX_ROWS, seq)
    steps = seq // rows
    pad_heads = lambda v: jnp.pad(v.astype(F32), (0, LANES - SSM_HEADS))[None]
    row = lambda b, c: b * steps + c
    col_blocks = range(Z_OFF // SSD_COL_BLOCK, PROJ_WIDTH // SSD_COL_BLOCK)
    proj_specs = [pl.BlockSpec((rows, SSD_COL_BLOCK), lambda b, c, n=n: (row(b, c), n)) for n in col_blocks]
    return pl.pallas_call(
        functools.partial(_ssd_kernel, rows=rows),
        grid=(batch, steps),
        in_specs=proj_specs + [
                  pl.BlockSpec((rows, LANES), lambda b, c: (row(b, c), 0)),
                  pl.BlockSpec((CONV_WIDTH, CONV_CH), lambda b, c: (0, 0)),
                  pl.BlockSpec((1, CONV_CH), lambda b, c: (0, 0)),
                  pl.BlockSpec((1, LANES), lambda b, c: (0, 0)),
                  pl.BlockSpec((1, LANES), lambda b, c: (0, 0)),
                  pl.BlockSpec((1, SSM_WIDTH), lambda b, c: (0, 0)),
                  pl.BlockSpec((1, SSM_WIDTH), lambda b, c: (0, 0))],
        out_specs=pl.BlockSpec((rows, SSM_WIDTH), lambda b, c: (row(b, c), 0)),
        out_shape=jax.ShapeDtypeStruct((batch * seq, SSM_WIDTH), BF16),
        scratch_shapes=[pltpu.VMEM((rows + BLK, CONV_CH), BF16),
                        pltpu.VMEM((BLK, CONV_CH), F32),
                        pltpu.VMEM((SSM_HEADS // 2, D_STATE, LANES), F32)],
        compiler_params=_params(2),
        name="ssd_mixer",
    )(*([proj] * len(proj_specs)), dt_raw, conv_w.astype(F32), conv_b.astype(F32)[None], pad_heads(dt_bias),
      pad_heads(a_log), jnp.repeat(d_skip.astype(F32), HEAD_DIM)[None], norm_w.astype(F32)[None])


def _outproj_kernel(x_ref, a_ref, s_ref, wa_ref, ws_ref, nw_ref, o_ref, hw_ref, r_ref):
    d = o_ref.shape[1]
    ss = jnp.zeros((o_ref.shape[0], 1), F32)
    for c0 in range(0, d, OUT_SLAB):
        cols = slice(c0, c0 + OUT_SLAB)
        x1 = x_ref[:, cols] + _dot(a_ref[...], wa_ref[:, cols]) + _dot(s_ref[...], ws_ref[:, cols])
        o_ref[:, cols] = x1
        hw_ref[:, cols] = (x1 * nw_ref[:, cols]).astype(BF16)
        ss = ss + jnp.sum(x1 * x1, axis=-1, keepdims=True)
    r_ref[...] = jnp.broadcast_to(lax.rsqrt(ss * (1.0 / d) + EPS), r_ref.shape)


def _out_proj(x, attn, ssm, w_out, ffn_norm_w, layer):
    m, d = x.shape
    tm = min(ROW_TILE, m)
    rows = lambda width: pl.BlockSpec((tm, width), lambda i: (i, 0))
    w_half = lambda half: pl.BlockSpec((None, w_out.shape[1] // 2, d), lambda i: (layer, half, 0),
                                       pipeline_mode=pl.Buffered(1))
    return pl.pallas_call(
        _outproj_kernel,
        grid=(m // tm,),
        in_specs=[rows(d), rows(ATTN_WIDTH), rows(SSM_WIDTH),
                  w_half(0), w_half(1), _resident((1, d))],
        out_specs=[rows(d), rows(d), rows(LANES)],
        out_shape=[jax.ShapeDtypeStruct((m, d), F32),
                   jax.ShapeDtypeStruct((m, d), BF16),
                   jax.ShapeDtypeStruct((m, LANES), F32)],
        compiler_params=_params(1),
        name="out_proj",
    )(x, attn, ssm, w_out, w_out, ffn_norm_w)


def _ffn_kernel(x_ref, hw_ref, r_ref, wg_ref, wu_ref, wd_ref, o_ref):
    def tile(first):
        hw = hw_ref[...]
        r = r_ref[...]
        gate = _dot(hw, wg_ref[...])
        up = _dot(hw, wu_ref[...])
        act = jnp.concatenate(
            [((gate[:, c:c + LANES] * up[:, c:c + LANES]) * (r * r)
              * _sigmoid(gate[:, c:c + LANES] * r)).astype(BF16)
             for c in range(0, gate.shape[1], LANES)], axis=1)
        down = _dot(act, wd_ref[...])
        if first:
            o_ref[...] = x_ref[...] + down
        else:
            o_ref[...] += down

    first_tile = pl.program_id(1) == 0
    pl.when(first_tile)(functools.partial(tile, True))
    pl.when(jnp.logical_not(first_tile))(functools.partial(tile, False))


def _ffn(x, hw, r, w_gate, w_up, w_down):
    m, d = x.shape
    f = w_gate.shape[1]
    tm, tf = min(FFN_ROW_TILE, m), FF_TILE
    rows = lambda width: pl.BlockSpec((tm, width), lambda i, j: (i, 0))
    return pl.pallas_call(
        _ffn_kernel,
        grid=(m // tm, f // tf),
        in_specs=[rows(d), rows(d), rows(LANES),
                  pl.BlockSpec((d, tf), lambda i, j: (0, j)),
                  pl.BlockSpec((d, tf), lambda i, j: (0, j)),
                  pl.BlockSpec((tf, d), lambda i, j: (j, 0))],
        out_specs=rows(d),
        out_shape=jax.ShapeDtypeStruct((m, d), F32),
        compiler_params=_params(2),
        name="swiglu_ffn",
    )(x, hw, r, w_gate, w_up, w_down)


def _layer(x, small, big, layer, *, batch, seq):
    (attn_norm_w, q_norm_w, k_norm_w, sinks, conv_w, conv_b, dt_bias, a_log, d_skip, ssm_norm_w, ffn_norm_w) = small
    w_in, w_dt, w_out, w_gate, w_up, w_down = big
    proj, dt_raw, wg, wu, wd = _norm_inproj(x, attn_norm_w.astype(F32)[None], w_in, w_dt,
                                            w_gate, w_up, w_down, layer)
    attn = _attention(proj, q_norm_w, k_norm_w, sinks, seq=seq)
    ssm = _ssd(proj, dt_raw, conv_w, conv_b, dt_bias, a_log, d_skip, ssm_norm_w, batch=batch, seq=seq)
    x, hw, r = _out_proj(x, attn, ssm, w_out, ffn_norm_w.astype(F32)[None], layer)
    return _ffn(x, hw, r, wg, wu, wd)


def kernel(x, attn_norm_w, w_in, q_norm_w, k_norm_w, sinks, conv_w, conv_b, dt_bias, a_log, d_skip,
           ssm_norm_w, w_out, ffn_norm_w, w_gate, w_up, w_down):
    batch, seq, d = x.shape
    big = (w_in.astype(BF16),
           jnp.pad(w_in[:, :, PROJ_WIDTH:], ((0, 0), (0, 0), (0, LANES - SSM_HEADS))).astype(BF16),
           w_out.astype(BF16), w_gate, w_up, w_down)
    stacked_small = (attn_norm_w, q_norm_w, k_norm_w, sinks, conv_w, conv_b, dt_bias, a_log, d_skip,
                     ssm_norm_w, ffn_norm_w)
    h = x.reshape(batch * seq, d)
    for layer in range(attn_norm_w.shape[0]):
        h = _layer(h, tuple(t[layer] for t in stacked_small), big, layer, batch=batch, seq=seq)
    return h.reshape(batch, seq, d)
```

```python
import functools

import numpy as np
import jax
import jax.numpy as jnp
from jax import lax
from jax.experimental import pallas as pl
from jax.experimental.pallas import tpu as pltpu

F32 = jnp.float32
BF16 = jnp.bfloat16

D_MODEL = 2048
HEAD_DIM = 64
ATTN_WIDTH = 1024
N_Q_HEADS = ATTN_WIDTH // HEAD_DIM
N_KV_HEADS = 4
Q_PER_KV = N_Q_HEADS // N_KV_HEADS
KV_WIDTH = N_KV_HEADS * HEAD_DIM
BLK = 128
SSM_WIDTH = 1024
SSM_HEADS = SSM_WIDTH // HEAD_DIM
SSM_GROUPS = 4
D_STATE = 128
CONV_WIDTH = 4
CONV_CH = SSM_WIDTH + 2 * SSM_GROUPS * D_STATE
D_FF = 5632
EPS = 1e-6

LANES = 128
SUBLANES = 8
BF16_SUBLANES = 16
HALF = LANES // 2
MXU_DIM = 256
VMEM_LIMIT = 60 * 1024 * 1024

K_OFF = ATTN_WIDTH
V_OFF = K_OFF + KV_WIDTH
Z_OFF = V_OFF + KV_WIDTH
XBC_OFF = Z_OFF + SSM_WIDTH
PROJ_WIDTH = XBC_OFF + CONV_CH
SSD_COL_BLOCK = 512

ROW_TILE = 512
INPROJ_COL_TILE = 1536
OUT_SLAB = 512
FFN_ROW_TILE = 1024
FF_TILE = 512
MIX_ROWS = 512
CONV_SLAB = 512

ALIBI_SLOPES = tuple(float(s) for s in np.power(
    np.float32(2.0), -8.0 * np.arange(1, N_Q_HEADS + 1, dtype=np.float32) / N_Q_HEADS))


def _params(n_axes):
    return pltpu.CompilerParams(dimension_semantics=("arbitrary",) * n_axes,
                                vmem_limit_bytes=VMEM_LIMIT)


def _split_bf16(x, parts):
    out = []
    for _ in range(parts - 1):
        hi = x.astype(BF16)
        out.append(hi)
        x = x - hi.astype(F32)
    out.append(x.astype(BF16))
    return out


def _dot(a, b):
    return jnp.dot(a, b, preferred_element_type=F32)


def _dot_nt(a, b):
    return lax.dot_general(a, b, (((1,), (1,)), ((), ())), preferred_element_type=F32)


def _norm_inproj_kernel(x_ref, nw_ref, w_ref, wdt_ref, wg_ref, wu_ref, wd_ref,
                        o_ref, odt_ref, wg_out, wu_out, wd_out, h_ref, *, down_steps):
    wg_out[...] = wg_ref[...].astype(BF16)
    wu_out[...] = wu_ref[...].astype(BF16)

    @pl.when(pl.program_id(0) < down_steps)
    def _():
        wd_out[...] = wd_ref[...].astype(BF16)

    x = x_ref[...]
    ms = jnp.mean(x * x, axis=-1, keepdims=True)
    h_ref[...] = (x * lax.rsqrt(ms + EPS) * nw_ref[...]).astype(BF16)
    odt_ref[...] = _dot(h_ref[...], wdt_ref[...])
    for n in range(o_ref.shape[1] // INPROJ_COL_TILE):
        cols = slice(n * INPROJ_COL_TILE, (n + 1) * INPROJ_COL_TILE)
        o_ref[:, cols] = _dot(h_ref[...], w_ref[:, cols]).astype(o_ref.dtype)


def _resident(shape, layer=None):
    if layer is None:
        return pl.BlockSpec(shape, lambda i: (0,) * len(shape), pipeline_mode=pl.Buffered(1))
    return pl.BlockSpec((None,) + shape, lambda i: (layer,) + (0,) * len(shape), pipeline_mode=pl.Buffered(1))


def _norm_inproj(x, norm_w, w_in, w_dt, w_gate, w_up, w_down, layer):
    m, d = x.shape
    n, f = PROJ_WIDTH, w_gate.shape[2]
    tm = min(ROW_TILE, m)
    steps = m // tm
    down_steps = steps
    while f % down_steps or (f // down_steps) % BF16_SUBLANES:
        down_steps //= 2
    up_rows, down_rows = d // steps, f // down_steps
    assert d % steps == 0 and up_rows % BF16_SUBLANES == 0
    down_blk = lambda i: jnp.minimum(i, down_steps - 1)
    return pl.pallas_call(
        functools.partial(_norm_inproj_kernel, down_steps=down_steps),
        grid=(steps,),
        in_specs=[pl.BlockSpec((tm, d), lambda i: (i, 0)),
                  _resident((1, d)),
                  _resident((d, n), layer),
                  _resident((d, LANES), layer),
                  pl.BlockSpec((None, up_rows, f), lambda i: (layer, i, 0)),
                  pl.BlockSpec((None, up_rows, f), lambda i: (layer, i, 0)),
                  pl.BlockSpec((None, down_rows, d), lambda i: (layer, down_blk(i), 0))],
        out_specs=[pl.BlockSpec((tm, n), lambda i: (i, 0)),
                   pl.BlockSpec((tm, LANES), lambda i: (i, 0)),
                   pl.BlockSpec((up_rows, f), lambda i: (i, 0)),
                   pl.BlockSpec((up_rows, f), lambda i: (i, 0)),
                   pl.BlockSpec((down_rows, d), lambda i: (down_blk(i), 0))],
        out_shape=[jax.ShapeDtypeStruct((m, n), BF16),
                   jax.ShapeDtypeStruct((m, LANES), F32),
                   jax.ShapeDtypeStruct((d, f), BF16),
                   jax.ShapeDtypeStruct((d, f), BF16),
                   jax.ShapeDtypeStruct((f, d), BF16)],
        scratch_shapes=[pltpu.VMEM((tm, d), BF16)],
        compiler_params=_params(1),
        name="norm_inproj",
    )(x, norm_w, w_in, w_dt, w_gate, w_up, w_down)


def _head_sumsq(x):
    r = lax.broadcasted_iota(jnp.int32, (MXU_DIM, MXU_DIM), 0) // HEAD_DIM
    c = lax.broadcasted_iota(jnp.int32, (MXU_DIM, MXU_DIM), 1) // HEAD_DIM
    seg = (r == c).astype(BF16)
    outs = []
    for g in range(x.shape[1] // MXU_DIM):
        xg = x[:, g * MXU_DIM:(g + 1) * MXU_DIM]
        hi, lo = _split_bf16(xg * xg, 2)
        outs.append(_dot(hi, seg) + _dot(lo, seg))
    return outs[0] if len(outs) == 1 else jnp.concatenate(outs, axis=1)


def _attn_kernel(sink_ref, q_ref, kc_ref, vc_ref, kp_ref, vp_ref, qw_ref, kw_ref, o_ref,
                 qs_ref, kd_ref, vd_ref, *, rows, blocks_per_seq):
    step = pl.program_id(0)
    nblk = rows // BLK
    lo_half = lax.broadcasted_iota(jnp.int32, (1, LANES), 1) < HALF

    k_all = jnp.concatenate([kp_ref[...], kc_ref[...]], axis=0).astype(F32)
    v_all = jnp.concatenate([vp_ref[...], vc_ref[...]], axis=0).astype(F32)
    k_all = k_all * lax.rsqrt(_head_sumsq(k_all) * (1.0 / HEAD_DIM) + EPS) * kw_ref[...]
    for src, dst in ((k_all, kd_ref), (v_all, vd_ref)):
        for p in range(KV_WIDTH // LANES):
            a = src[:, p * LANES:(p + 1) * LANES]
            b = pltpu.roll(a, HALF, 1)
            dst[2 * p] = jnp.where(lo_half, a, b).astype(BF16)
            dst[2 * p + 1] = jnp.where(lo_half, b, a).astype(BF16)

    q = q_ref[...].astype(F32)
    q = q * lax.rsqrt(_head_sumsq(q) * (1.0 / HEAD_DIM) + EPS) * qw_ref[...]
    lo_q = (lax.broadcasted_iota(jnp.int32, (1, ATTN_WIDTH), 1) & HALF) == 0
    qs_ref[0] = jnp.where(lo_q, q, 0.0).astype(BF16)
    qs_ref[1] = jnp.where(lo_q, 0.0, q).astype(BF16)

    r_idx = lax.broadcasted_iota(jnp.int32, (BLK, 2 * BLK), 0)
    c_idx = lax.broadcasted_iota(jnp.int32, (BLK, 2 * BLK), 1)
    band = (c_idx > r_idx) & (c_idx <= r_idx + BLK)
    row_f = lax.broadcasted_iota(jnp.int32, (BLK, 1), 0).astype(F32)
    col_f = lax.broadcasted_iota(jnp.int32, (1, 2 * BLK), 1).astype(F32)

    def block(t, carry):
        r0 = pl.multiple_of(t * BLK, BLK)
        first = ((step * nblk + t) % blocks_per_seq) == 0
        valid = band & (c_idx >= jnp.where(first, BLK, 0))
        for j in range(N_KV_HEADS):
            c0 = j * Q_PER_KV * HEAD_DIM
            lhs = jnp.concatenate(
                [qs_ref[half, pl.ds(r0, BLK), c0 + grp * LANES:c0 + (grp + 1) * LANES]
                 for grp in range(2) for half in range(2)], axis=0)
            kj = kd_ref[j, pl.ds(r0, 2 * BLK), :]
            vj = vd_ref[j, pl.ds(r0, 2 * BLK), :]
            s = _dot_nt(lhs, kj)
            probs, inv_den = [], []
            for hh in range(Q_PER_KV):
                h = j * Q_PER_KV + hh
                slope = ALIBI_SLOPES[h]
                sh = s[hh * BLK:(hh + 1) * BLK] + slope * col_f
                sh = jnp.where(valid, sh, -jnp.inf)
                sink = sink_ref[h] + slope * (row_f + float(BLK))
                mx = jnp.maximum(jnp.max(sh, axis=-1, keepdims=True), sink)
                p = jnp.exp(sh - mx)
                den = jnp.sum(p, axis=-1, keepdims=True) + jnp.exp(sink - mx)
                probs.append(p.astype(BF16))
                inv_den.append(1.0 / den)
            o = _dot(jnp.concatenate(probs, axis=0), vj)
            for grp in range(2):
                oa = o[(2 * grp) * BLK:(2 * grp + 1) * BLK] * inv_den[2 * grp]
                ob = o[(2 * grp + 1) * BLK:(2 * grp + 2) * BLK] * inv_den[2 * grp + 1]
                o_ref[pl.ds(r0, BLK), c0 + grp * LANES:c0 + (grp + 1) * LANES] = (
                    jnp.where(lo_half, oa, ob).astype(o_ref.dtype))
        return carry

    lax.fori_loop(0, nblk, block, 0, unroll=2)


def _attention(proj, q_norm_w, k_norm_w, sinks, *, seq):
    m = proj.shape[0]
    rows = min(MIX_ROWS, seq)
    nblk = rows // BLK
    qw = (jnp.tile(q_norm_w.astype(F32), N_Q_HEADS) * (HEAD_DIM ** -0.5))[None]
    kw = jnp.tile(k_norm_w.astype(F32), N_KV_HEADS)[None]
    kb, vb = K_OFF // KV_WIDTH, V_OFF // KV_WIDTH
    prev = lambda i: jnp.maximum(i * nblk - 1, 0)
    return pl.pallas_call(
        functools.partial(_attn_kernel, rows=rows, blocks_per_seq=seq // BLK),
        grid=(m // rows,),
        in_specs=[pl.BlockSpec(memory_space=pltpu.SMEM),
                  pl.BlockSpec((rows, ATTN_WIDTH), lambda i: (i, 0)),
                  pl.BlockSpec((rows, KV_WIDTH), lambda i: (i, kb)),
                  pl.BlockSpec((rows, KV_WIDTH), lambda i: (i, vb)),
                  pl.BlockSpec((BLK, KV_WIDTH), lambda i: (prev(i), kb)),
                  pl.BlockSpec((BLK, KV_WIDTH), lambda i: (prev(i), vb)),
                  pl.BlockSpec((1, ATTN_WIDTH), lambda i: (0, 0)),
                  pl.BlockSpec((1, KV_WIDTH), lambda i: (0, 0))],
        out_specs=pl.BlockSpec((rows, ATTN_WIDTH), lambda i: (i, 0)),
        out_shape=jax.ShapeDtypeStruct((m, ATTN_WIDTH), BF16),
        scratch_shapes=[pltpu.VMEM((2, rows, ATTN_WIDTH), BF16),
                        pltpu.VMEM((N_KV_HEADS, rows + BLK, LANES), BF16),
                        pltpu.VMEM((N_KV_HEADS, rows + BLK, LANES), BF16)],
        compiler_params=_params(1),
        name="swa_attention",
    )(sinks.astype(F32), proj, proj, proj, proj, proj, qw, kw)


def _sigmoid(x):
    return 1.0 / (1.0 + jnp.exp(-x))


def _ssd_kernel(*refs, rows):
    nz, nx = SSM_WIDTH // SSD_COL_BLOCK, CONV_CH // SSD_COL_BLOCK
    z_refs, xbc_refs = refs[:nz], refs[nz:nz + nx]
    (dt_ref, cw_ref, cb_ref, dtb_ref, alog_ref, dskip_ref, nw_ref, o_ref,
     xwin_ref, u_ref, h_ref) = refs[nz + nx:]
    nchunk = rows // BLK
    first_step = pl.program_id(1) == 0

    @pl.when(first_step)
    def _():
        xwin_ref[0:BLK, :] = jnp.zeros((BLK, CONV_CH), BF16)
        h_ref[...] = jnp.zeros_like(h_ref)

    @pl.when(jnp.logical_not(first_step))
    def _():
        xwin_ref[0:BLK, :] = xwin_ref[rows:rows + BLK, :]

    for n, ref in enumerate(xbc_refs):
        xwin_ref[BLK:BLK + rows, n * SSD_COL_BLOCK:(n + 1) * SSD_COL_BLOCK] = ref[...]

    out_t = lax.broadcasted_iota(jnp.int32, (BLK, 2 * BLK), 0)
    in_t = lax.broadcasted_iota(jnp.int32, (BLK, 2 * BLK), 1)
    shifts = [(in_t == out_t + (BLK - (CONV_WIDTH - 1 - k))).astype(BF16) for k in range(CONV_WIDTH)]

    lane = lax.broadcasted_iota(jnp.int32, (1, LANES), 1)
    lo_half = lane < HALF
    r_idx = lax.broadcasted_iota(jnp.int32, (BLK, BLK), 0)
    c_idx = lax.broadcasted_iota(jnp.int32, (BLK, BLK), 1)
    causal = c_idx <= r_idx
    tri = causal.astype(BF16)
    expand = (lax.broadcasted_iota(jnp.int32, (LANES, SSM_WIDTH), 0)
              == lax.broadcasted_iota(jnp.int32, (LANES, SSM_WIDTH), 1) // HEAD_DIM).astype(BF16)
    a_neg = jnp.where(lane < SSM_HEADS, -jnp.exp(alog_ref[...]), 0.0)

    def chunk(t, carry):
        r0 = pl.multiple_of(t * BLK, BLK)
        rws = pl.ds(r0, BLK)

        for c0 in range(0, CONV_CH, CONV_SLAB):
            cs = slice(c0, c0 + CONV_SLAB)
            win = xwin_ref[pl.ds(r0, 2 * BLK), cs]
            acc = cb_ref[:, cs]
            for k in range(CONV_WIDTH):
                acc = acc + cw_ref[k:k + 1, cs] * _dot(shifts[k], win)
            u_ref[:, cs] = acc * _sigmoid(acc)

        v = dt_ref[rws, :] + dtb_ref[...]
        dt = jnp.maximum(v, 0.0) + jnp.log1p(jnp.exp(-jnp.abs(v)))
        da = dt * a_neg
        cum = sum(_dot(tri, part) for part in _split_bf16(da, 3))
        cum_t = cum.T[0:SSM_HEADS]
        dt_t = dt.T[0:SSM_HEADS]
        w_t = jnp.exp(cum_t[:, BLK - 1:BLK] - cum_t) * dt_t
        p_hi, p_lo = _split_bf16(jnp.exp(cum), 2)
        p_exp = _dot(p_hi, expand) + _dot(p_lo, expand)

        for g in range(SSM_GROUPS):
            b_g = u_ref[:, SSM_WIDTH + g * D_STATE:SSM_WIDTH + (g + 1) * D_STATE]
            c_g = u_ref[:, SSM_WIDTH + (SSM_GROUPS + g) * D_STATE:SSM_WIDTH + (SSM_GROUPS + g + 1) * D_STATE]
            c_bf = c_g.astype(BF16)
            cb = _dot_nt(c_bf, b_g.astype(BF16))
            b_t = b_g.T
            gated = []
            for pair in range(2 * g, 2 * g + 2):
                cols = slice(pair * LANES, (pair + 1) * LANES)
                x_f = u_ref[:, cols]
                x_bf = x_f.astype(BF16)
                y_k, s_k = [], []
                for k in (2 * pair, 2 * pair + 1):
                    seg = cum[:, k:k + 1] - cum_t[k:k + 1, :]
                    decay = jnp.exp(jnp.where(causal, seg, -jnp.inf))
                    y_k.append(_dot((cb * decay * dt_t[k:k + 1, :]).astype(BF16), x_bf))
                    s_k.append(_dot((b_t * w_t[k:k + 1, :]).astype(BF16), x_bf))
                h_prev = h_ref[pair]
                y = (jnp.where(lo_half, y_k[0], y_k[1])
                     + _dot(c_bf, h_prev.astype(BF16)) * p_exp[:, cols]
                     + dskip_ref[:, cols] * x_f)
                h_ref[pair] = h_prev * p_exp[BLK - 1:BLK, cols] + jnp.where(lo_half, s_k[0], s_k[1])
                zb, zc = divmod(pair * LANES, SSD_COL_BLOCK)
                zf = z_refs[zb][rws, zc:zc + LANES].astype(F32)
                gated.append(y * (zf * _sigmoid(zf)))
            gg = jnp.concatenate(gated, axis=1)
            ms = jnp.mean(gg * gg, axis=-1, keepdims=True)
            gcols = slice(2 * g * LANES, (2 * g + 2) * LANES)
            o_ref[rws, gcols] = (gg * lax.rsqrt(ms + EPS) * nw_ref[:, gcols]).astype(o_ref.dtype)
        return carry

    lax.fori_loop(0, nchunk, chunk, 0, unroll=2)


def _ssd(proj, dt_raw, conv_w, conv_b, dt_bias, a_log, d_skip, norm_w, *, batch, seq):
    rows = min(MIX_ROWS, seq)
    steps = seq // rows
    pad_heads = lambda v: jnp.pad(v.astype(F32), (0, LANES - SSM_HEADS))[None]
    row = lambda b, c: b * steps + c
    col_blocks = range(Z_OFF // SSD_COL_BLOCK, PROJ_WIDTH // SSD_COL_BLOCK)
    proj_specs = [pl.BlockSpec((rows, SSD_COL_BLOCK), lambda b, c, n=n: (row(b, c), n)) for n in col_blocks]
    return pl.pallas_call(
        functools.partial(_ssd_kernel, rows=rows),
        grid=(batch, steps),
        in_specs=proj_specs + [
                  pl.BlockSpec((rows, LANES), lambda b, c: (row(b, c), 0)),
                  pl.BlockSpec((CONV_WIDTH, CONV_CH), lambda b, c: (0, 0)),
                  pl.BlockSpec((1, CONV_CH), lambda b, c: (0, 0)),
                  pl.BlockSpec((1, LANES), lambda b, c: (0, 0)),
                  pl.BlockSpec((1, LANES), lambda b, c: (0, 0)),
                  pl.BlockSpec((1, SSM_WIDTH), lambda b, c: (0, 0)),
                  pl.BlockSpec((1, SSM_WIDTH), lambda b, c: (0, 0))],
        out_specs=pl.BlockSpec((rows, SSM_WIDTH), lambda b, c: (row(b, c), 0)),
        out_shape=jax.ShapeDtypeStruct((batch * seq, SSM_WIDTH), BF16),
        scratch_shapes=[pltpu.VMEM((rows + BLK, CONV_CH), BF16),
                        pltpu.VMEM((BLK, CONV_CH), F32),
                        pltpu.VMEM((SSM_HEADS // 2, D_STATE, LANES), F32)],
        compiler_params=_params(2),
        name="ssd_mixer",
    )(*([proj] * len(proj_specs)), dt_raw, conv_w.astype(F32), conv_b.astype(F32)[None], pad_heads(dt_bias),
      pad_heads(a_log), jnp.repeat(d_skip.astype(F32), HEAD_DIM)[None], norm_w.astype(F32)[None])


def _outproj_kernel(x_ref, a_ref, s_ref, wa_ref, ws_ref, nw_ref, o_ref, hw_ref, r_ref):
    d = o_ref.shape[1]
    ss = jnp.zeros((o_ref.shape[0], 1), F32)
    for c0 in range(0, d, OUT_SLAB):
        cols = slice(c0, c0 + OUT_SLAB)
        x1 = x_ref[:, cols] + _dot(a_ref[...], wa_ref[:, cols]) + _dot(s_ref[...], ws_ref[:, cols])
        o_ref[:, cols] = x1
        hw_ref[:, cols] = (x1 * nw_ref[:, cols]).astype(BF16)
        ss = ss + jnp.sum(x1 * x1, axis=-1, keepdims=True)
    r_ref[...] = jnp.broadcast_to(lax.rsqrt(ss * (1.0 / d) + EPS), r_ref.shape)


def _out_proj(x, attn, ssm, w_out, ffn_norm_w, layer):
    m, d = x.shape
    tm = min(2 * ROW_TILE, m)
    rows = lambda width: pl.BlockSpec((tm, width), lambda i: (i, 0))
    w_half = lambda half: pl.BlockSpec((None, w_out.shape[1] // 2, d), lambda i: (layer, half, 0),
                                       pipeline_mode=pl.Buffered(1))
    return pl.pallas_call(
        _outproj_kernel,
        grid=(m // tm,),
        in_specs=[rows(d), rows(ATTN_WIDTH), rows(SSM_WIDTH),
                  w_half(0), w_half(1), _resident((1, d))],
        out_specs=[rows(d), rows(d), rows(LANES)],
        out_shape=[jax.ShapeDtypeStruct((m, d), F32),
                   jax.ShapeDtypeStruct((m, d), BF16),
                   jax.ShapeDtypeStruct((m, LANES), F32)],
        compiler_params=_params(1),
        name="out_proj",
    )(x, attn, ssm, w_out, w_out, ffn_norm_w)


def _ffn_kernel(x_ref, hw_ref, r_ref, wg_ref, wu_ref, wd_ref, o_ref):
    def tile(first):
        hw = hw_ref[...]
        r = r_ref[...]
        gate = _dot(hw, wg_ref[...])
        up = _dot(hw, wu_ref[...])
        act = jnp.concatenate(
            [((gate[:, c:c + LANES] * up[:, c:c + LANES]) * (r * r)
              * _sigmoid(gate[:, c:c + LANES] * r)).astype(BF16)
             for c in range(0, gate.shape[1], LANES)], axis=1)
        down = _dot(act, wd_ref[...])
        if first:
            o_ref[...] = x_ref[...] + down
        else:
            o_ref[...] += down

    first_tile = pl.program_id(1) == 0
    pl.when(first_tile)(functools.partial(tile, True))
    pl.when(jnp.logical_not(first_tile))(functools.partial(tile, False))


def _ffn(x, hw, r, w_gate, w_up, w_down):
    m, d = x.shape
    f = w_gate.shape[1]
    tm, tf = min(FFN_ROW_TILE, m), FF_TILE
    rows = lambda width: pl.BlockSpec((tm, width), lambda i, j: (i, 0))
    return pl.pallas_call(
        _ffn_kernel,
        grid=(m // tm, f // tf),
        in_specs=[rows(d), rows(d), rows(LANES),
                  pl.BlockSpec((d, tf), lambda i, j: (0, j)),
                  pl.BlockSpec((d, tf), lambda i, j: (0, j)),
                  pl.BlockSpec((tf, d), lambda i, j: (j, 0))],
        out_specs=rows(d),
        out_shape=jax.ShapeDtypeStruct((m, d), F32),
        compiler_params=_params(2),
        name="swiglu_ffn",
    )(x, hw, r, w_gate, w_up, w_down)


def _layer(x, small, big, layer, *, batch, seq):
    (attn_norm_w, q_norm_w, k_norm_w, sinks, conv_w, conv_b, dt_bias, a_log, d_skip, ssm_norm_w, ffn_norm_w) = small
    w_in, w_dt, w_out, w_gate, w_up, w_down = big
    proj, dt_raw, wg, wu, wd = _norm_inproj(x, attn_norm_w.astype(F32)[None], w_in, w_dt,
                                            w_gate, w_up, w_down, layer)
    attn = _attention(proj, q_norm_w, k_norm_w, sinks, seq=seq)
    ssm = _ssd(proj, dt_raw, conv_w, conv_b, dt_bias, a_log, d_skip, ssm_norm_w, batch=batch, seq=seq)
    x, hw, r = _out_proj(x, attn, ssm, w_out, ffn_norm_w.astype(F32)[None], layer)
    return _ffn(x, hw, r, wg, wu, wd)


def kernel(x, attn_norm_w, w_in, q_norm_w, k_norm_w, sinks, conv_w, conv_b, dt_bias, a_log, d_skip,
           ssm_norm_w, w_out, ffn_norm_w, w_gate, w_up, w_down):
    batch, seq, d = x.shape
    big = (w_in.astype(BF16),
           jnp.pad(w_in[:, :, PROJ_WIDTH:], ((0, 0), (0, 0), (0, LANES - SSM_HEADS))).astype(BF16),
           w_out.astype(BF16), w_gate, w_up, w_down)
    stacked_small = (attn_norm_w, q_norm_w, k_norm_w, sinks, conv_w, conv_b, dt_bias, a_log, d_skip,
                     ssm_norm_w, ffn_norm_w)
    h = x.reshape(batch * seq, d)
    for layer in range(attn_norm_w.shape[0]):
        h = _layer(h, tuple(t[layer] for t in stacked_small), big, layer, batch=batch, seq=seq)
    return h.reshape(batch, seq, d)
```
